```python
import jax, jax.numpy as jnp
from jax import lax
import numpy as np

D_MODEL = 1024
BATCH = 4
SEQ = 4096
DEPTH = 4

N_MIXERS = 2
SB_HEADS = 16
SB_HEAD_DIM = D_MODEL // SB_HEADS
SB_BLOCK = 128
RET_HEADS = 4
RET_QK_DIM = D_MODEL // RET_HEADS
RET_V_DIM = 2 * D_MODEL // RET_HEADS
RET_CHUNK = 128
ROPE_BASE = 10000.0
MLP_HIDDEN = 4 * D_MODEL
RMS_EPS = 1e-6
GN_EPS = 1e-6
N_SB_LAYERS = (DEPTH + N_MIXERS - 1) // N_MIXERS
N_RET_LAYERS = DEPTH // N_MIXERS

kernel_name = "hybrid_stickbreaking_retention_trunk"

F32 = jnp.float32


def rms_norm(x, g):
    xf = x.astype(F32)
    y = xf * lax.rsqrt(jnp.mean(xf * xf, axis=-1, keepdims=True) + RMS_EPS)
    return (y * g.astype(F32)).astype(x.dtype)


def stick_breaking_attention(h, w_in, w_out):
    b, s, _ = h.shape
    q, k, v = jnp.split(h @ w_in, 3, axis=-1)

    def heads(t):
        return t.reshape(b, s, SB_HEADS, SB_HEAD_DIM).transpose(0, 2, 1, 3).astype(F32)

    q, k, v = heads(q), heads(k), heads(v)
    scale = SB_HEAD_DIM ** -0.5
    outs = []
    for blk in range(s // SB_BLOCK):
        start = blk * SB_BLOCK
        end = start + SB_BLOCK
        qb = q[:, :, start:end]
        kp = k[:, :, :end]
        vp = v[:, :, :end]
        z = jnp.einsum('bhqd,bhkd->bhqk', qb, kp) * scale
        t_idx = start + jnp.arange(SB_BLOCK)[:, None]
        s_idx = jnp.arange(end)[None, :]
        mask = s_idx < t_idx
        log_keep = jnp.where(mask, jax.nn.log_sigmoid(-z), 0.0)
        later = lax.cumsum(log_keep, axis=3, reverse=True) - log_keep
        a = jnp.where(mask, jnp.exp(jax.nn.log_sigmoid(z) + later), 0.0)
        outs.append(jnp.einsum('bhqk,bhkd->bhqd', a, vp))
    o = jnp.concatenate(outs, axis=2)
    o = o.transpose(0, 2, 1, 3).reshape(b, s, D_MODEL).astype(h.dtype)
    return o @ w_out


def rotary(x, pos):
    half = x.shape[-1] // 2
    inv_freq = 1.0 / (ROPE_BASE ** jnp.linspace(0.0, 1.0, half, dtype=F32))
    ang = pos.astype(F32)[:, None] * inv_freq[None, :]
    cos, sin = jnp.cos(ang), jnp.sin(ang)
    x1, x2 = x[..., :half], x[..., half:]
    return jnp.concatenate([x1 * cos - x2 * sin, x1 * sin + x2 * cos], axis=-1)


def retention(h, w_in, w_out):
    b, s, _ = h.shape
    c = RET_CHUNK
    nc = s // c
    q, k, v, g = jnp.split(h @ w_in, [D_MODEL, 2 * D_MODEL, 4 * D_MODEL], axis=-1)
    q = q.reshape(b, s, RET_HEADS, RET_QK_DIM).transpose(0, 2, 1, 3).astype(F32)
    k = k.reshape(b, s, RET_HEADS, RET_QK_DIM).transpose(0, 2, 1, 3).astype(F32)
    v = v.reshape(b, s, RET_HEADS, RET_V_DIM).transpose(0, 2, 1, 3).astype(F32)
    pos = jnp.arange(s)
    q = rotary(q, pos)
    k = rotary(k, pos) * (RET_QK_DIM ** -0.5)

    log_gamma = jnp.log1p(-jnp.exp2(-5.0 - jnp.arange(RET_HEADS, dtype=F32)))
    idx = jnp.arange(c, dtype=F32)
    rel = idx[:, None] - idx[None, :]
    decay_in = jnp.where(rel >= 0, jnp.exp(log_gamma[:, None, None] * jnp.maximum(rel, 0.0)), 0.0)
    q_decay = jnp.exp(log_gamma[:, None] * (idx + 1.0))[None, :, :, None]
    k_decay = jnp.exp(log_gamma[:, None] * (c - 1.0 - idx))[None, :, :, None]
    chunk_decay = jnp.exp(log_gamma * c)[None, :, None, None]

    qc = q.reshape(b, RET_HEADS, nc, c, RET_QK_DIM)
    kc = k.reshape(b, RET_HEADS, nc, c, RET_QK_DIM)
    vc = v.reshape(b, RET_HEADS, nc, c, RET_V_DIM)

    scores = jnp.einsum('bhnid,bhnjd->bhnij', qc, kc) * decay_in[None, :, None]
    inner = jnp.einsum('bhnij,bhnje->bhnie', scores, vc)

    def step(state, xs):
        q_n, k_n, v_n = xs
        cross = jnp.einsum('bhid,bhde->bhie', q_n, state) * q_decay
        state = state * chunk_decay + jnp.einsum('bhjd,bhje->bhde', k_n * k_decay, v_n)
        return state, cross

    state0 = jnp.zeros((b, RET_HEADS, RET_QK_DIM, RET_V_DIM), F32)
    xs = (qc.transpose(2, 0, 1, 3, 4), kc.transpose(2, 0, 1, 3, 4), vc.transpose(2, 0, 1, 3, 4))
    _, cross = lax.scan(step, state0, xs)
    o = (inner + cross.transpose(1, 2, 0, 3, 4)).reshape(b, RET_HEADS, s, RET_V_DIM)

    mu = jnp.mean(o, axis=-1, keepdims=True)
    var = jnp.mean(jnp.square(o - mu), axis=-1, keepdims=True)
    o = (o - mu) * lax.rsqrt(var + GN_EPS)
    o = o.transpose(0, 2, 1, 3).reshape(b, s, RET_HEADS * RET_V_DIM)
    y = jax.nn.silu(g.astype(F32)) * o
    return y.astype(h.dtype) @ w_out


def squared_relu_mlp(h, w_up, w_down):
    u = jax.nn.relu(h @ w_up)
    return (u * u) @ w_down


def setup_inputs(seed: int = 0) -> dict:
    key = jax.random.key(seed)
    ks = jax.random.split(key, 11)

    def w(k, shape, fan_in):
        return jax.random.normal(k, shape, F32) * (fan_in ** -0.5)

    def gain(k, shape):
        return 1.0 + 0.05 * jax.random.normal(k, shape, F32)

    return {
        "x": jax.random.normal(ks[0], (BATCH, SEQ, D_MODEL), F32),
        "w_sb_in": w(ks[1], (N_SB_LAYERS, D_MODEL, 3 * D_MODEL), D_MODEL),
        "w_sb_out": w(ks[2], (N_SB_LAYERS, D_MODEL, D_MODEL), D_MODEL),
        "w_ret_in": w(ks[3], (N_RET_LAYERS, D_MODEL, 6 * D_MODEL), D_MODEL),
        "w_ret_out": w(ks[4], (N_RET_LAYERS, 2 * D_MODEL, D_MODEL), 2 * D_MODEL),
        "g_mix": gain(ks[5], (DEPTH, D_MODEL)),
        "g_mlp": gain(ks[6], (DEPTH, D_MODEL)),
        "w_mlp_up": w(ks[7], (DEPTH, D_MODEL, MLP_HIDDEN), D_MODEL),
        "w_mlp_down": w(ks[8], (DEPTH, MLP_HIDDEN, D_MODEL), MLP_HIDDEN),
        "g_final": gain(ks[9], (D_MODEL,)),
    }


def reference(x, w_sb_in, w_sb_out, w_ret_in, w_ret_out, g_mix, g_mlp, w_mlp_up, w_mlp_down, g_final):
    h = x
    for layer in range(DEPTH):
        hn = rms_norm(h, g_mix[layer])
        i = layer // N_MIXERS
        if layer % N_MIXERS == 0:
            h = h + stick_breaking_attention(hn, w_sb_in[i], w_sb_out[i])
        else:
            h = h + retention(hn, w_ret_in[i], w_ret_out[i])
        h = h + squared_relu_mlp(rms_norm(h, g_mlp[layer]), w_mlp_up[layer], w_mlp_down[layer])
    return rms_norm(h, g_final)
```

```python
import functools

import jax
import jax.numpy as jnp
from jax import lax
from jax.experimental import pallas as pl
from jax.experimental.pallas import tpu as pltpu

F32 = jnp.float32
BF16 = jnp.bfloat16

D_MODEL = 1024
N_MIXERS = 2
SB_HEADS = 16
SB_HEAD_DIM = D_MODEL // SB_HEADS
RET_HEADS = 4
RET_QK_DIM = D_MODEL // RET_HEADS
RET_V_DIM = 2 * D_MODEL // RET_HEADS
RET_CHUNK = 128
ROPE_BASE = 10000.0
MLP_HIDDEN = 4 * D_MODEL
RMS_EPS = 1e-6
GN_EPS = 1e-6

LANES = 128
VMEM_LIMIT = 56 * 1024 * 1024

ROW_TILE = 512
SB_TQ = 256
SB_KB = 256


def _rms_norm_f32(x, g):
    ms = jnp.mean(x * x, axis=-1, keepdims=True)
    return x * lax.rsqrt(ms + RMS_EPS) * g


def _norm_matmul_kernel(x_ref, g_ref, w_ref, o_ref, *, n_chunk):
    xn = _rms_norm_f32(x_ref[...], g_ref[...]).astype(BF16)
    for j in range(w_ref.shape[1] // n_chunk):
        cols = slice(j * n_chunk, (j + 1) * n_chunk)
        o_ref[:, cols] = jnp.dot(
            xn, w_ref[:, cols], preferred_element_type=F32).astype(o_ref.dtype)


def _norm_matmul(x, g, w, out_dtype):
    t, d = x.shape
    n = w.shape[1]
    return pl.pallas_call(
        functools.partial(_norm_matmul_kernel, n_chunk=1024),
        grid=(t // ROW_TILE,),
        in_specs=[
            pl.BlockSpec((ROW_TILE, d), lambda i: (i, 0)),
            pl.BlockSpec((1, d), lambda i: (0, 0)),
            pl.BlockSpec((d, n), lambda i: (0, 0)),
        ],
        out_specs=pl.BlockSpec((ROW_TILE, n), lambda i: (i, 0)),
        out_shape=jax.ShapeDtypeStruct((t, n), out_dtype),
        compiler_params=pltpu.CompilerParams(
            dimension_semantics=("parallel",), vmem_limit_bytes=VMEM_LIMIT),
        name="norm_matmul",
    )(x, g.reshape(1, d), w)


def _matmul_res_kernel(a_ref, w_ref, r_ref, o_ref):
    o_ref[...] = r_ref[...] + jnp.dot(
        a_ref[...], w_ref[...], preferred_element_type=F32)


def _matmul_residual(a, w, res):
    t, k = a.shape
    n = w.shape[1]
    return pl.pallas_call(
        _matmul_res_kernel,
        grid=(t // ROW_TILE,),
        in_specs=[
            pl.BlockSpec((ROW_TILE, k), lambda i: (i, 0)),
            pl.BlockSpec((k, n), lambda i: (0, 0)),
            pl.BlockSpec((ROW_TILE, n), lambda i: (i, 0)),
        ],
        out_specs=pl.BlockSpec((ROW_TILE, n), lambda i: (i, 0)),
        out_shape=jax.ShapeDtypeStruct((t, n), F32),
        compiler_params=pltpu.CompilerParams(
            dimension_semantics=("parallel",), vmem_limit_bytes=VMEM_LIMIT),
        name="matmul_residual",
    )(a, w, res)


def _mlp_kernel(x_ref, g_ref, wu_ref, wd_ref, go_ref, o_ref, *, h_chunk, final_norm):
    x = x_ref[...]
    xn = _rms_norm_f32(x, g_ref[...]).astype(BF16)
    y = x
    for j in range(wu_ref.shape[1] // h_chunk):
        cols = slice(j * h_chunk, (j + 1) * h_chunk)
        u = jnp.maximum(jnp.dot(xn, wu_ref[:, cols], preferred_element_type=F32), 0.0)
        y = y + jnp.dot((u * u).astype(BF16), wd_ref[cols, :],
                        preferred_element_type=F32)
    if final_norm:
        y = _rms_norm_f32(y, go_ref[...])
    o_ref[...] = y


def _mlp(x, g, w_up, w_down, g_out, final_norm):
    t, d = x.shape
    hid = w_up.shape[1]
    return pl.pallas_call(
        functools.partial(_mlp_kernel, h_chunk=1024, final_norm=final_norm),
        grid=(t // ROW_TILE,),
        in_specs=[
            pl.BlockSpec((ROW_TILE, d), lambda i: (i, 0)),
            pl.BlockSpec((1, d), lambda i: (0, 0)),
            pl.BlockSpec((d, hid), lambda i: (0, 0)),
            pl.BlockSpec((hid, d), lambda i: (0, 0)),
            pl.BlockSpec((1, d), lambda i: (0, 0)),
        ],
        out_specs=pl.BlockSpec((ROW_TILE, d), lambda i: (i, 0)),
        out_shape=jax.ShapeDtypeStruct((t, d), F32),
        compiler_params=pltpu.CompilerParams(
            dimension_semantics=("parallel",), vmem_limit_bytes=VMEM_LIMIT),
        name="mlp",
    )(x, g.reshape(1, d), w_up, w_down, g_out.reshape(1, d))


def _sb_attn_kernel(q_ref, k_ref, v_ref, tri_ref, o_ref, acc_ref, r_ref):
    tq, kb = SB_TQ, SB_KB
    i = pl.program_id(2)
    qf = q_ref[...].astype(F32) * (SB_HEAD_DIM ** -0.5)
    lane = lax.broadcasted_iota(jnp.int32, (tq, LANES), 1)
    row = lax.broadcasted_iota(jnp.int32, (tq, kb), 0)
    col = lax.broadcasted_iota(jnp.int32, (tq, kb), 1)
    causal = col < row

    outs = []
    for h2 in range(LANES // SB_HEAD_DIM):
        in_head = (lane >= h2 * SB_HEAD_DIM) & (lane < (h2 + 1) * SB_HEAD_DIM)
        qh = jnp.where(in_head, qf, 0.0).astype(BF16)

        def block(kstart, masked):
            kblk = k_ref[pl.ds(kstart, kb), :]
            vblk = v_ref[pl.ds(kstart, kb), :]
            z = lax.dot_general(qh, kblk, (((1,), (1,)), ((), ())),
                                preferred_element_type=F32)
            sp = jnp.maximum(z, 0.0) + jnp.log(1.0 + jnp.exp(-jnp.abs(z)))
            if masked:
                sp = jnp.where(causal, sp, 0.0)
            hi = sp.astype(BF16)
            lo = (sp - hi.astype(F32)).astype(BF16)
            c = jnp.dot(jnp.concatenate([hi, lo], axis=1), tri_ref[...],
                        preferred_element_type=F32)
            r = r_ref[...]
            c = c + jnp.concatenate([r] * (kb // LANES), axis=1)
            a = jnp.exp(z - c)
            if masked:
                a = jnp.where(causal, a, 0.0)
            acc_ref[...] += jnp.dot(a.astype(BF16), vblk, preferred_element_type=F32)
            r_ref[...] = jnp.broadcast_to(c[:, 0:1], (tq, LANES))

        acc_ref[...] = jnp.zeros_like(acc_ref)
        r_ref[...] = jnp.zeros_like(r_ref)
        block(pl.multiple_of(i * kb, kb), True)

        def body(jj, carry):
            block(pl.multiple_of((i - 1 - jj) * kb, kb), False)
            return carry

        lax.fori_loop(0, i, body, 0)
        outs.append(acc_ref[...])

    o_ref[...] = jnp.where(lane < SB_HEAD_DIM, outs[0], outs[1]).astype(o_ref.dtype)


def _sb_attention(qkv, batch, seq):
    t = qkv.shape[0]
    nq = seq // SB_TQ
    pairs = D_MODEL // LANES
    j = jnp.arange(SB_KB)
    tri = (j[:, None] >= j[None, :]).astype(BF16)
    tri2 = jnp.concatenate([tri, tri], axis=0)
    return pl.pallas_call(
        _sb_attn_kernel,
        grid=(batch, pairs, nq),
        in_specs=[
            pl.BlockSpec((SB_TQ, LANES), lambda b, p, i: (b * nq + i, p)),
            pl.BlockSpec((seq, LANES), lambda b, p, i: (b, pairs + p)),
            pl.BlockSpec((seq, LANES), lambda b, p, i: (b, 2 * pairs + p)),
            pl.BlockSpec((2 * SB_KB, SB_KB), lambda b, p, i: (0, 0)),
        ],
        out_specs=pl.BlockSpec((SB_TQ, LANES), lambda b, p, i: (b * nq + i, p)),
        out_shape=jax.ShapeDtypeStruct((t, D_MODEL), BF16),
        scratch_shapes=[
            pltpu.VMEM((SB_TQ, LANES), F32),
            pltpu.VMEM((SB_TQ, LANES), F32),
        ],
        compiler_params=pltpu.CompilerParams(
            dimension_semantics=("parallel", "parallel", "arbitrary"),
            vmem_limit_bytes=VMEM_LIMIT),
        name="sb_attention",
    )(qkv, qkv, qkv, tri2)


def _rotate(x, cos, sin):
    half = x.shape[-1] // 2
    x1, x2 = x[:, :half], x[:, half:]
    return jnp.concatenate([x1 * cos - x2 * sin, x1 * sin + x2 * cos], axis=1)


def _retention_kernel(q_ref, k_ref, v_ref, g_ref, cos_ref, sin_ref, din_ref,
                      qd_ref, kd_ref, cd_ref, o_ref, state_ref):
    n = pl.program_id(2)

    @pl.when(n == 0)
    def _():
        state_ref[...] = jnp.zeros_like(state_ref)

    cos, sin = cos_ref[...], sin_ref[...]
    q = _rotate(q_ref[...], cos, sin)
    k = _rotate(k_ref[...], cos, sin) * (RET_QK_DIM ** -0.5)
    qb = q.astype(BF16)
    kb = k.astype(BF16)
    vb = v_ref[...].astype(BF16)

    scores = lax.dot_general(qb, kb, (((1,), (1,)), ((), ())),
                             preferred_element_type=F32) * din_ref[0]
    inner = jnp.dot(scores.astype(BF16), vb, preferred_element_type=F32)
    state = state_ref[...]
    cross = jnp.dot(qb, state.astype(BF16), preferred_element_type=F32) * qd_ref[0]
    kd = (k * kd_ref[0]).astype(BF16)
    state_ref[...] = state * cd_ref[0] + lax.dot_general(
        kd, vb, (((0,), (0,)), ((), ())), preferred_element_type=F32)

    o = inner + cross
    mu = jnp.mean(o, axis=-1, keepdims=True)
    xc = o - mu
    var = jnp.mean(xc * xc, axis=-1, keepdims=True)
    o = xc * lax.rsqrt(var + GN_EPS)
    g = g_ref[...]
    y = g * (1.0 / (1.0 + jnp.exp(-g))) * o
    o_ref[...] = y.astype(o_ref.dtype)


def _retention_constants(seq):
    c = RET_CHUNK
    half = RET_QK_DIM // 2
    inv_freq = 1.0 / (ROPE_BASE ** jnp.linspace(0.0, 1.0, half, dtype=F32))
    ang = jnp.arange(seq).astype(F32)[:, None] * inv_freq[None, :]
    cos, sin = jnp.cos(ang), jnp.sin(ang)

    log_gamma = jnp.log1p(-jnp.exp2(-5.0 - jnp.arange(RET_HEADS, dtype=F32)))
    idx = jnp.arange(c, dtype=F32)
    rel = idx[:, None] - idx[None, :]
    decay_in = jnp.where(
        rel >= 0, jnp.exp(log_gamma[:, None, None] * jnp.maximum(rel, 0.0)), 0.0)
    q_decay = jnp.exp(log_gamma[:, None] * (idx + 1.0))
    k_decay = jnp.exp(log_gamma[:, None] * (c - 1.0 - idx))
    chunk_decay = jnp.exp(log_gamma * c)
    qd = jnp.broadcast_to(q_decay[:, :, None], (RET_HEADS, c, RET_V_DIM))
    kd = jnp.broadcast_to(k_decay[:, :, None], (RET_HEADS, c, RET_QK_DIM))
    cd = jnp.broadcast_to(chunk_decay[:, None, None], (RET_HEADS, 1, RET_V_DIM))
    return cos, sin, decay_in, qd, kd, cd


def _retention(qkvg, batch, seq):
    t = qkvg.shape[0]
    c = RET_CHUNK
    nc = seq // c
    cos, sin, decay_in, qd, kd, cd = _retention_constants(seq)
    kq = D_MODEL // RET_QK_DIM
    kv = 2 * D_MODEL // RET_V_DIM
    kg = 4 * D_MODEL // RET_V_DIM
    half = RET_QK_DIM // 2
    return pl.pallas_call(
        _retention_kernel,
        grid=(batch, RET_HEADS, nc),
        in_specs=[
            pl.BlockSpec((c, RET_QK_DIM), lambda b, h, n: (b * nc + n, h)),
            pl.BlockSpec((c, RET_QK_DIM), lambda b, h, n: (b * nc + n, kq + h)),
            pl.BlockSpec((c, RET_V_DIM), lambda b, h, n: (b * nc + n, kv + h)),
            pl.BlockSpec((c, RET_V_DIM), lambda b, h, n: (b * nc + n, kg + h)),
            pl.BlockSpec((c, half), lambda b, h, n: (n, 0)),
            pl.BlockSpec((c, half), lambda b, h, n: (n, 0)),
            pl.BlockSpec((1, c, c), lambda b, h, n: (h, 0, 0)),
            pl.BlockSpec((1, c, RET_V_DIM), lambda b, h, n: (h, 0, 0)),
            pl.BlockSpec((1, c, RET_QK_DIM), lambda b, h, n: (h, 0, 0)),
            pl.BlockSpec((1, 1, RET_V_DIM), lambda b, h, n: (h, 0, 0)),
        ],
        out_specs=pl.BlockSpec((c, RET_V_DIM), lambda b, h, n: (b * nc + n, h)),
        out_shape=jax.ShapeDtypeStruct((t, RET_HEADS * RET_V_DIM), BF16),
        scratch_shapes=[pltpu.VMEM((RET_QK_DIM, RET_V_DIM), F32)],
        compiler_params=pltpu.CompilerParams(
            dimension_semantics=("parallel", "parallel", "arbitrary"),
            vmem_limit_bytes=VMEM_LIMIT),
        name="retention",
    )(qkvg, qkvg, qkvg, qkvg, cos, sin, decay_in, qd, kd, cd)


def kernel(x, w_sb_in, w_sb_out, w_ret_in, w_ret_out, g_mix, g_mlp, w_mlp_up,
           w_mlp_down, g_final):
    batch, seq, d = x.shape
    depth = g_mix.shape[0]
    h = x.reshape(batch * seq, d)
    for layer in range(depth):
        i = layer // N_MIXERS
        if layer % N_MIXERS == 0:
            qkv = _norm_matmul(h, g_mix[layer], w_sb_in[i].astype(BF16), BF16)
            o = _sb_attention(qkv, batch, seq)
            h = _matmul_residual(o, w_sb_out[i].astype(BF16), h)
        else:
            qkvg = _norm_matmul(h, g_mix[layer], w_ret_in[i].astype(BF16), F32)
            y = _retention(qkvg, batch, seq)
            h = _matmul_residual(y, w_ret_out[i].astype(BF16), h)
        h = _mlp(h, g_mlp[layer], w_mlp_up[layer].astype(BF16),
                 w_mlp_down[layer].astype(BF16), g_final, layer == depth - 1)
    return h.reshape(batch, seq, d)
```

```python
import functools

import jax
import jax.numpy as jnp
from jax import lax
from jax.experimental import pallas as pl
from jax.experimental.pallas import tpu as pltpu

F32 = jnp.float32
BF16 = jnp.bfloat16

D_MODEL = 1024
N_MIXERS = 2
SB_HEADS = 16
SB_HEAD_DIM = D_MODEL // SB_HEADS
RET_HEADS = 4
RET_QK_DIM = D_MODEL // RET_HEADS
RET_V_DIM = 2 * D_MODEL // RET_HEADS
ROPE_BASE = 10000.0
MLP_HIDDEN = 4 * D_MODEL
RMS_EPS = 1e-6
GN_EPS = 1e-6
LOG2_E = 1.4426950408889634

LANES = 128
VMEM_LIMIT = 56 * 1024 * 1024

ROW_TILE = 512
SB_TQ = 256
SB_KB = 256


def _rms_norm_f32(x, g):
    ms = jnp.mean(x * x, axis=-1, keepdims=True)
    return x * lax.rsqrt(ms + RMS_EPS) * g


def _norm_matmul_kernel(x_ref, g_ref, w_ref, o_ref, *, n_chunk):
    xn = _rms_norm_f32(x_ref[...], g_ref[...]).astype(BF16)
    for j in range(w_ref.shape[1] // n_chunk):
        cols = slice(j * n_chunk, (j + 1) * n_chunk)
        o_ref[:, cols] = jnp.dot(
            xn, w_ref[:, cols], preferred_element_type=F32).astype(o_ref.dtype)


def _norm_matmul(x, g, w, out_dtype):
    t, d = x.shape
    n = w.shape[1]
    return pl.pallas_call(
        functools.partial(_norm_matmul_kernel, n_chunk=D_MODEL),
        grid=(t // ROW_TILE,),
        in_specs=[
            pl.BlockSpec((ROW_TILE, d), lambda i: (i, 0)),
            pl.BlockSpec((1, d), lambda i: (0, 0)),
            pl.BlockSpec((d, n), lambda i: (0, 0)),
        ],
        out_specs=pl.BlockSpec((ROW_TILE, n), lambda i: (i, 0)),
        out_shape=jax.ShapeDtypeStruct((t, n), out_dtype),
        compiler_params=pltpu.CompilerParams(
            dimension_semantics=("parallel",), vmem_limit_bytes=VMEM_LIMIT),
        name="norm_matmul",
    )(x, g.reshape(1, d), w)


def _matmul_res_kernel(a_ref, w_ref, r_ref, o_ref):
    o_ref[...] = r_ref[...] + jnp.dot(
        a_ref[...], w_ref[...], preferred_element_type=F32)


def _matmul_residual(a, w, res):
    t, k = a.shape
    n = w.shape[1]
    return pl.pallas_call(
        _matmul_res_kernel,
        grid=(t // ROW_TILE,),
        in_specs=[
            pl.BlockSpec((ROW_TILE, k), lambda i: (i, 0)),
            pl.BlockSpec((k, n), lambda i: (0, 0)),
            pl.BlockSpec((ROW_TILE, n), lambda i: (i, 0)),
        ],
        out_specs=pl.BlockSpec((ROW_TILE, n), lambda i: (i, 0)),
        out_shape=jax.ShapeDtypeStruct((t, n), F32),
        compiler_params=pltpu.CompilerParams(
            dimension_semantics=("parallel",), vmem_limit_bytes=VMEM_LIMIT),
        name="matmul_residual",
    )(a, w, res)


def _mlp_kernel(x_ref, g_ref, wu_ref, wd_ref, go_ref, o_ref, *, h_chunk, final_norm):
    x = x_ref[...]
    xn = _rms_norm_f32(x, g_ref[...]).astype(BF16)
    y = x
    for j in range(wu_ref.shape[1] // h_chunk):
        cols = slice(j * h_chunk, (j + 1) * h_chunk)
        u = jnp.maximum(jnp.dot(xn, wu_ref[:, cols], preferred_element_type=F32), 0.0)
        y = y + jnp.dot((u * u).astype(BF16), wd_ref[cols, :],
                        preferred_element_type=F32)
    if final_norm:
        y = _rms_norm_f32(y, go_ref[...])
    o_ref[...] = y


def _mlp(x, g, w_up, w_down, g_out, final_norm):
    t, d = x.shape
    hid = w_up.shape[1]
    return pl.pallas_call(
        functools.partial(_mlp_kernel, h_chunk=1024, final_norm=final_norm),
        grid=(t // ROW_TILE,),
        in_specs=[
            pl.BlockSpec((ROW_TILE, d), lambda i: (i, 0)),
            pl.BlockSpec((1, d), lambda i: (0, 0)),
            pl.BlockSpec((d, hid), lambda i: (0, 0)),
            pl.BlockSpec((hid, d), lambda i: (0, 0)),
            pl.BlockSpec((1, d), lambda i: (0, 0)),
        ],
        out_specs=pl.BlockSpec((ROW_TILE, d), lambda i: (i, 0)),
        out_shape=jax.ShapeDtypeStruct((t, d), F32),
        compiler_params=pltpu.CompilerParams(
            dimension_semantics=("parallel",), vmem_limit_bytes=VMEM_LIMIT),
        name="mlp",
    )(x, g.reshape(1, d), w_up, w_down, g_out.reshape(1, d))


SB_DONE_SUM = 104.0


def _sb_tile(qh, k_ref, v_ref, tri, kstart, r, mask):
    kblk = k_ref[pl.ds(kstart, SB_KB), :]
    vblk = v_ref[pl.ds(kstart, SB_KB), :]
    z = lax.dot_general(qh, kblk, (((1,), (1,)), ((), ())),
                        preferred_element_type=F32)
    log1pe = jnp.log(1.0 + jnp.exp2(jnp.abs(z) * (-LOG2_E)))
    sp = jnp.maximum(z, 0.0) + log1pe
    log_sig = jnp.minimum(z, 0.0) - log1pe
    if mask is not None:
        sp = jnp.where(mask, sp, 0.0)
    c = jnp.dot(sp.astype(BF16), tri, preferred_element_type=F32)
    if r is not None:
        c = c + jnp.concatenate([r] * (SB_KB // LANES), axis=1)
    a = jnp.exp(log_sig - c)
    if mask is not None:
        a = jnp.where(mask, a, 0.0)
    pv = jnp.dot(a.astype(BF16), vblk, preferred_element_type=F32)
    r_new = jnp.broadcast_to(c[:, 0:1] + sp[:, 0:1], (SB_TQ, LANES))
    return pv, r_new


def _sb_attn_kernel(q_ref, k_ref, v_ref, tri_ref, o_ref, acc_ref, r_ref):
    tq, kb = SB_TQ, SB_KB
    n_heads = LANES // SB_HEAD_DIM
    i = pl.program_id(2)
    qf = q_ref[...].astype(F32) * (SB_HEAD_DIM ** -0.5)
    lane = lax.broadcasted_iota(jnp.int32, (tq, LANES), 1)
    tri = tri_ref[...]

    def head_q(h2):
        in_head = (lane >= h2 * SB_HEAD_DIM) & (lane < (h2 + 1) * SB_HEAD_DIM)
        return jnp.where(in_head, qf, 0.0).astype(BF16)

    qhs = [head_q(h2) for h2 in range(n_heads)]

    def diagonal_tiles(with_previous):
        row = lax.broadcasted_iota(jnp.int32, (tq, kb), 0)
        col = lax.broadcasted_iota(jnp.int32, (tq, kb), 1)
        causal = col < row
        for h2 in range(n_heads):
            pv, r = _sb_tile(qhs[h2], k_ref, v_ref, tri,
                             pl.multiple_of(i * kb, kb), None, causal)
            if with_previous:
                pv2, r = _sb_tile(qhs[h2], k_ref, v_ref, tri,
                                  pl.multiple_of((i - 1) * kb, kb), r, None)
                pv = pv + pv2
            acc_ref[h2] = pv
            r_ref[h2] = r

    @pl.when(i == 0)
    def _():
        diagonal_tiles(False)

    @pl.when(i > 0)
    def _():
        diagonal_tiles(True)

    def min_sum():
        return jnp.min(r_ref[...])

    def cond(carry):
        j, m = carry
        return jnp.logical_and(j >= 0, m < SB_DONE_SUM)

    def body(carry):
        j, _ = carry
        for h2 in range(n_heads):
            pv, r = _sb_tile(qhs[h2], k_ref, v_ref, tri,
                             pl.multiple_of(j * kb, kb), r_ref[h2], None)
            acc_ref[h2] += pv
            r_ref[h2] = r
        return j - 1, min_sum()

    lax.while_loop(cond, body, (i - 2, min_sum()))

    o_ref[...] = jnp.where(lane < SB_HEAD_DIM, acc_ref[0], acc_ref[1]).astype(o_ref.dtype)


def _sb_attention(qkv, batch, seq):
    t = qkv.shape[0]
    nq = seq // SB_TQ
    pairs = D_MODEL // LANES
    j = jnp.arange(SB_KB)
    tri = (j[:, None] > j[None, :]).astype(BF16)
    return pl.pallas_call(
        _sb_attn_kernel,
        grid=(batch, pairs, nq),
        in_specs=[
            pl.BlockSpec((SB_TQ, LANES), lambda b, p, i: (b * nq + i, p)),
            pl.BlockSpec((seq, LANES), lambda b, p, i: (b, pairs + p)),
            pl.BlockSpec((seq, LANES), lambda b, p, i: (b, 2 * pairs + p)),
            pl.BlockSpec((SB_KB, SB_KB), lambda b, p, i: (0, 0)),
        ],
        out_specs=pl.BlockSpec((SB_TQ, LANES), lambda b, p, i: (b * nq + i, p)),
        out_shape=jax.ShapeDtypeStruct((t, D_MODEL), BF16),
        scratch_shapes=[
            pltpu.VMEM((LANES // SB_HEAD_DIM, SB_TQ, LANES), F32),
            pltpu.VMEM((LANES // SB_HEAD_DIM, SB_TQ, LANES), F32),
        ],
        compiler_params=pltpu.CompilerParams(
            dimension_semantics=("parallel", "parallel", "arbitrary"),
            vmem_limit_bytes=VMEM_LIMIT),
        name="sb_attention",
    )(qkv, qkv, qkv, tri)


RET_BLOCK = 256
RET_HEADS_PER_STEP = 2


def _rotate(x, cos, sin):
    half = x.shape[-1] // 2
    x1, x2 = x[:, :half], x[:, half:]
    return jnp.concatenate([x1 * cos - x2 * sin, x1 * sin + x2 * cos], axis=1)


def _retention_kernel(q_ref, k_ref, v_ref, g_ref, cos_ref, sin_ref, din_ref,
                      qd_ref, kd_ref, cd_ref, o_ref, state_ref):
    n = pl.program_id(2)

    @pl.when(n == 0)
    def _():
        state_ref[...] = jnp.zeros_like(state_ref)

    cos, sin = cos_ref[...], sin_ref[...]
    for hh in range(RET_HEADS_PER_STEP):
        qk_cols = slice(hh * RET_QK_DIM, (hh + 1) * RET_QK_DIM)
        v_cols = slice(hh * RET_V_DIM, (hh + 1) * RET_V_DIM)
        q = _rotate(q_ref[:, qk_cols], cos, sin)
        k = _rotate(k_ref[:, qk_cols], cos, sin) * (RET_QK_DIM ** -0.5)
        qb = q.astype(BF16)
        kb = k.astype(BF16)
        vb = v_ref[:, v_cols].astype(BF16)

        scores = lax.dot_general(qb, kb, (((1,), (1,)), ((), ())),
                                 preferred_element_type=F32) * din_ref[hh]
        inner = jnp.dot(scores.astype(BF16), vb, preferred_element_type=F32)
        state = state_ref[hh]
        cross = jnp.dot(qb, state.astype(BF16), preferred_element_type=F32) * qd_ref[hh]
        kd = (k * kd_ref[hh]).astype(BF16)
        state_ref[hh] = state * cd_ref[hh] + lax.dot_general(
            kd, vb, (((0,), (0,)), ((), ())), preferred_element_type=F32)

        o = inner + cross
        mu = jnp.mean(o, axis=-1, keepdims=True)
        xc = o - mu
        var = jnp.mean(xc * xc, axis=-1, keepdims=True)
        o = xc * lax.rsqrt(var + GN_EPS)
        g = g_ref[:, v_cols]
        y = g * (1.0 / (1.0 + jnp.exp(-g))) * o
        o_ref[:, v_cols] = y.astype(o_ref.dtype)


def _retention_constants(seq):
    c = RET_BLOCK
    half = RET_QK_DIM // 2
    inv_freq = 1.0 / (ROPE_BASE ** jnp.linspace(0.0, 1.0, half, dtype=F32))
    ang = jnp.arange(seq).astype(F32)[:, None] * inv_freq[None, :]
    cos, sin = jnp.cos(ang), jnp.sin(ang)

    log_gamma = jnp.log1p(-jnp.exp2(-5.0 - jnp.arange(RET_HEADS, dtype=F32)))
    idx = jnp.arange(c, dtype=F32)
    rel = idx[:, None] - idx[None, :]
    decay_in = jnp.where(
        rel >= 0, jnp.exp(log_gamma[:, None, None] * jnp.maximum(rel, 0.0)), 0.0)
    q_decay = jnp.exp(log_gamma[:, None] * (idx + 1.0))
    k_decay = jnp.exp(log_gamma[:, None] * (c - 1.0 - idx))
    chunk_decay = jnp.exp(log_gamma * c)
    qd = jnp.broadcast_to(q_decay[:, :, None], (RET_HEADS, c, RET_V_DIM))
    kd = jnp.broadcast_to(k_decay[:, :, None], (RET_HEADS, c, RET_QK_DIM))
    cd = jnp.broadcast_to(chunk_decay[:, None, None], (RET_HEADS, 1, RET_V_DIM))
    return cos, sin, decay_in, qd, kd, cd


def _retention(qkvg, batch, seq):
    t = qkvg.shape[0]
    c = RET_BLOCK
    nc = seq // c
    hs = RET_HEADS_PER_STEP
    qk_w, v_w = hs * RET_QK_DIM, hs * RET_V_DIM
    cos, sin, decay_in, qd, kd, cd = _retention_constants(seq)
    kq = D_MODEL // qk_w
    kv = 2 * D_MODEL // v_w
    kg = 4 * D_MODEL // v_w
    half = RET_QK_DIM // 2
    return pl.pallas_call(
        _retention_kernel,
        grid=(batch, RET_HEADS // hs, nc),
        in_specs=[
            pl.BlockSpec((c, qk_w), lambda b, h, n: (b * nc + n, h)),
            pl.BlockSpec((c, qk_w), lambda b, h, n: (b * nc + n, kq + h)),
            pl.BlockSpec((c, v_w), lambda b, h, n: (b * nc + n, kv + h)),
            pl.BlockSpec((c, v_w), lambda b, h, n: (b * nc + n, kg + h)),
            pl.BlockSpec((c, half), lambda b, h, n: (n, 0)),
            pl.BlockSpec((c, half), lambda b, h, n: (n, 0)),
            pl.BlockSpec((hs, c, c), lambda b, h, n: (h, 0, 0)),
            pl.BlockSpec((hs, c, RET_V_DIM), lambda b, h, n: (h, 0, 0)),
            pl.BlockSpec((hs, c, RET_QK_DIM), lambda b, h, n: (h, 0, 0)),
            pl.BlockSpec((hs, 1, RET_V_DIM), lambda b, h, n: (h, 0, 0)),
        ],
        out_specs=pl.BlockSpec((c, v_w), lambda b, h, n: (b * nc + n, h)),
        out_shape=jax.ShapeDtypeStruct((t, RET_HEADS * RET_V_DIM), BF16),
        scratch_shapes=[pltpu.VMEM((hs, RET_QK_DIM, RET_V_DIM), F32)],
        compiler_params=pltpu.CompilerParams(
            dimension_semantics=("parallel", "parallel", "arbitrary"),
            vmem_limit_bytes=VMEM_LIMIT),
        name="retention",
    )(qkvg, qkvg, qkvg, qkvg, cos, sin, decay_in, qd, kd, cd)


def kernel(x, w_sb_in, w_sb_out, w_ret_in, w_ret_out, g_mix, g_mlp, w_mlp_up,
           w_mlp_down, g_final):
    batch, seq, d = x.shape
    depth = g_mix.shape[0]
    h = x.reshape(batch * seq, d)
    for layer in range(depth):
        i = layer // N_MIXERS
        if layer % N_MIXERS == 0:
            qkv = _norm_matmul(h, g_mix[layer], w_sb_in[i].astype(BF16), BF16)
            o = _sb_attention(qkv, batch, seq)
            h = _matmul_residual(o, w_sb_out[i].astype(BF16), h)
        else:
            qkvg = _norm_matmul(h, g_mix[layer], w_ret_in[i].astype(BF16), F32)
            y = _retention(qkvg, batch, seq)
            h = _matmul_residual(y, w_ret_out[i].astype(BF16), h)
        h = _mlp(h, g_mlp[layer], w_mlp_up[layer].astype(BF16),
                 w_mlp_down[layer].astype(BF16), g_final, layer == depth - 1)
    return h.reshape(batch, seq, d)
```

```python
import functools

import jax
import jax.numpy as jnp
from jax import lax
from jax.experimental import pallas as pl
from jax.experimental.pallas import tpu as pltpu

F32 = jnp.float32
BF16 = jnp.bfloat16

D_MODEL = 1024
N_MIXERS = 2
SB_HEADS = 16
SB_HEAD_DIM = D_MODEL // SB_HEADS
RET_HEADS = 4
RET_QK_DIM = D_MODEL // RET_HEADS
RET_V_DIM = 2 * D_MODEL // RET_HEADS
ROPE_BASE = 10000.0
MLP_HIDDEN = 4 * D_MODEL
RMS_EPS = 1e-6
GN_EPS = 1e-6
LOG2_E = 1.4426950408889634

LANES = 128
VMEM_LIMIT = 56 * 1024 * 1024

ROW_TILE = 512
CAST_ROWS = 256
SB_TQ = 256
SB_KB = 256
SB_LANE_BLOCKS = 2
RET_BLOCK = 256
RET_HEADS_PER_STEP = 2


def _rms_norm_f32(x, g):
    ms = jnp.mean(x * x, axis=-1, keepdims=True)
    return x * lax.rsqrt(ms + RMS_EPS) * g


def _resident(block_shape, index_map):
    return pl.BlockSpec(block_shape, index_map, pipeline_mode=pl.Buffered(1))


def _params(*semantics):
    return pltpu.CompilerParams(dimension_semantics=semantics,
                                vmem_limit_bytes=VMEM_LIMIT)


def _cast_kernel(x_ref, o_ref):
    o_ref[...] = x_ref[...].astype(o_ref.dtype)


def _cast_bf16(w):
    layers, k, n = w.shape
    spec = pl.BlockSpec((None, CAST_ROWS, n), lambda a, r: (a, r, 0))
    return pl.pallas_call(
        _cast_kernel,
        grid=(layers, k // CAST_ROWS),
        in_specs=[spec],
        out_specs=spec,
        out_shape=jax.ShapeDtypeStruct(w.shape, BF16),
        compiler_params=_params("parallel", "parallel"),
        name="cast_bf16",
    )(w)


def _sb_proj_kernel(x_ref, g_ref, w_ref, o_ref):
    xn = _rms_norm_f32(x_ref[...], g_ref[...]).astype(BF16)
    for j in range(w_ref.shape[1] // D_MODEL):
        cols = slice(j * D_MODEL, (j + 1) * D_MODEL)
        o_ref[:, cols] = jnp.dot(
            xn, w_ref[:, cols], preferred_element_type=F32).astype(o_ref.dtype)


def _sb_proj(x, g, w, layer):
    t, d = x.shape
    n = w.shape[2]
    return pl.pallas_call(
        _sb_proj_kernel,
        grid=(t // ROW_TILE,),
        in_specs=[
            pl.BlockSpec((ROW_TILE, d), lambda i: (i, 0)),
            _resident((1, d), lambda i: (0, 0)),
            _resident((None, d, n), lambda i: (layer, 0, 0)),
        ],
        out_specs=pl.BlockSpec((ROW_TILE, n), lambda i: (i, 0)),
        out_shape=jax.ShapeDtypeStruct((t, n), BF16),
        compiler_params=_params("parallel"),
        name="sb_proj",
    )(x, g.reshape(1, d), w)


def _rotate(x, cos, sin):
    half = x.shape[-1] // 2
    x1, x2 = x[:, :half], x[:, half:]
    return jnp.concatenate([x1 * cos - x2 * sin, x1 * sin + x2 * cos], axis=1)


def _ret_proj_kernel(x_ref, g_ref, w_ref, cos_ref, sin_ref, kd_ref,
                     qk_ref, v_ref, sg_ref):
    xn = _rms_norm_f32(x_ref[...], g_ref[...]).astype(BF16)
    cos, sin = cos_ref[...], sin_ref[...]

    def proj(j):
        return jnp.dot(xn, w_ref[:, j * D_MODEL:(j + 1) * D_MODEL],
                       preferred_element_type=F32)

    q = proj(0)
    for hh in range(RET_HEADS):
        cols = slice(hh * RET_QK_DIM, (hh + 1) * RET_QK_DIM)
        qk_ref[:, cols] = _rotate(q[:, cols], cos, sin).astype(BF16)
    k = proj(1)
    for hh in range(RET_HEADS):
        cols = slice(hh * RET_QK_DIM, (hh + 1) * RET_QK_DIM)
        kr = _rotate(k[:, cols], cos, sin) * (RET_QK_DIM ** -0.5)
        qk_ref[:, D_MODEL + hh * RET_QK_DIM:D_MODEL + (hh + 1) * RET_QK_DIM] = kr.astype(BF16)
        qk_ref[:, 2 * D_MODEL + hh * RET_QK_DIM:2 * D_MODEL + (hh + 1) * RET_QK_DIM] = (
            kr * kd_ref[hh]).astype(BF16)
    for j in range(2):
        cols = slice(j * D_MODEL, (j + 1) * D_MODEL)
        v_ref[:, cols] = proj(2 + j).astype(BF16)
        gate = proj(4 + j)
        sg_ref[:, cols] = (gate * (1.0 / (1.0 + jnp.exp(-gate)))).astype(BF16)


def _retention_constants(seq):
    c = RET_BLOCK
    half = RET_QK_DIM // 2
    inv_freq = 1.0 / (ROPE_BASE ** jnp.linspace(0.0, 1.0, half, dtype=F32))
    ang = jnp.arange(seq).astype(F32)[:, None] * inv_freq[None, :]
    cos, sin = jnp.cos(ang), jnp.sin(ang)

    log_gamma = jnp.log1p(-jnp.exp2(-5.0 - jnp.arange(RET_HEADS, dtype=F32)))
    idx = jnp.arange(c, dtype=F32)
    rel = idx[:, None] - idx[None, :]
    decay_in = jnp.where(
        rel >= 0, jnp.exp(log_gamma[:, None, None] * jnp.maximum(rel, 0.0)), 0.0)
    q_decay = jnp.exp(log_gamma[:, None] * (idx + 1.0))
    k_decay = jnp.exp(log_gamma[:, None] * (c - 1.0 - idx))
    chunk_decay = jnp.exp(log_gamma * c)
    qd = jnp.broadcast_to(q_decay[:, :, None], (RET_HEADS, c, RET_V_DIM))
    k_decay_tile = jnp.tile(k_decay, (1, ROW_TILE // c))
    kd = jnp.broadcast_to(k_decay_tile[:, :, None], (RET_HEADS, ROW_TILE, RET_QK_DIM))
    cd = jnp.broadcast_to(chunk_decay[:, None, None], (RET_HEADS, 1, RET_V_DIM))
    return cos, sin, decay_in, qd, kd, cd


def _ret_proj(x, g, w, layer, cos, sin, kd, seq):
    t, d = x.shape
    n = w.shape[2]
    half = RET_QK_DIM // 2
    pos_blocks = seq // ROW_TILE
    row = lambda i: (i, 0)
    return pl.pallas_call(
        _ret_proj_kernel,
        grid=(t // ROW_TILE,),
        in_specs=[
            pl.BlockSpec((ROW_TILE, d), row),
            _resident((1, d), lambda i: (0, 0)),
            _resident((None, d, n), lambda i: (layer, 0, 0)),
            pl.BlockSpec((ROW_TILE, half), lambda i: (i % pos_blocks, 0)),
            pl.BlockSpec((ROW_TILE, half), lambda i: (i % pos_blocks, 0)),
            _resident((RET_HEADS, ROW_TILE, RET_QK_DIM), lambda i: (0, 0, 0)),
        ],
        out_specs=[
            pl.BlockSpec((ROW_TILE, 3 * d), row),
            pl.BlockSpec((ROW_TILE, 2 * d), row),
            pl.BlockSpec((ROW_TILE, 2 * d), row),
        ],
        out_shape=[
            jax.ShapeDtypeStruct((t, 3 * d), BF16),
            jax.ShapeDtypeStruct((t, 2 * d), BF16),
            jax.ShapeDtypeStruct((t, 2 * d), BF16),
        ],
        compiler_params=_params("parallel"),
        name="ret_proj",
    )(x, g.reshape(1, d), w, cos, sin, kd)


def _mlp_kernel(a_ref, wo_ref, r_ref, g_ref, wu_ref, wd_ref, go_ref, o_ref, *,
                h_chunk, final_norm):
    x = r_ref[...] + jnp.dot(a_ref[...], wo_ref[...], preferred_element_type=F32)
    xn = _rms_norm_f32(x, g_ref[...]).astype(BF16)
    y = x
    for j in range(wu_ref.shape[1] // h_chunk):
        cols = slice(j * h_chunk, (j + 1) * h_chunk)
        u = jnp.maximum(jnp.dot(xn, wu_ref[:, cols], preferred_element_type=F32), 0.0)
        y = y + jnp.dot((u * u).astype(BF16), wd_ref[cols, :],
                        preferred_element_type=F32)
    if final_norm:
        y = _rms_norm_f32(y, go_ref[...])
    o_ref[...] = y


def _out_proj_mlp(a, w_out, out_layer, res, g, w_up, w_down, layer, g_out, final_norm):
    t, d = res.shape
    k = a.shape[1]
    hid = w_up.shape[2]
    row = lambda i: (i, 0)
    return pl.pallas_call(
        functools.partial(_mlp_kernel, h_chunk=D_MODEL, final_norm=final_norm),
        grid=(t // ROW_TILE,),
        in_specs=[
            pl.BlockSpec((ROW_TILE, k), row),
            _resident((None, k, d), lambda i: (out_layer, 0, 0)),
            pl.BlockSpec((ROW_TILE, d), row),
            _resident((1, d), lambda i: (0, 0)),
            _resident((None, d, hid), lambda i: (layer, 0, 0)),
            _resident((None, hid, d), lambda i: (layer, 0, 0)),
            _resident((1, d), lambda i: (0, 0)),
        ],
        out_specs=pl.BlockSpec((ROW_TILE, d), row),
        out_shape=jax.ShapeDtypeStruct((t, d), F32),
        compiler_params=_params("parallel"),
        name="out_proj_mlp",
    )(a, w_out, res, g.reshape(1, d), w_up, w_down, g_out.reshape(1, d))


SB_DONE_SUM = 104.0
SB_HEADS_PER_LANE_BLOCK = LANES // SB_HEAD_DIM


SB_ROWS = SB_HEADS_PER_LANE_BLOCK * SB_TQ


def _sb_scores(q, keys):
    return lax.dot_general(q, keys, (((1,), (1,)), ((), ())),
                           preferred_element_type=F32)


def _sb_key_block(zs, vblks, tri, rs, mask):
    sps, log_sigs = [], []
    for z in zs:
        log1pe = jnp.log(1.0 + jnp.exp2(jnp.abs(z) * (-LOG2_E)))
        sp = jnp.maximum(z, 0.0) + log1pe
        log_sigs.append(z - sp)
        sps.append(sp if mask is None else jnp.where(mask, sp, 0.0))
    c_all = jnp.dot(jnp.concatenate([sp.astype(BF16) for sp in sps], axis=0), tri,
                    preferred_element_type=F32)
    pvs, rs_new = [], []
    for lb, (sp, log_sig, vblk) in enumerate(zip(sps, log_sigs, vblks)):
        c = c_all[lb * SB_ROWS:(lb + 1) * SB_ROWS]
        if rs is not None:
            c = c + jnp.concatenate([rs[lb]] * (SB_KB // LANES), axis=1)
        a = jnp.exp(log_sig - c)
        if mask is not None:
            a = jnp.where(mask, a, 0.0)
        pvs.append(jnp.dot(a.astype(BF16), vblk, preferred_element_type=F32))
        rs_new.append(jnp.broadcast_to(c[:, 0:1] + sp[:, 0:1], (SB_ROWS, LANES)))
    return pvs, rs_new


def _sb_attn_kernel(q_ref, k_ref, v_ref, tri_ref, o_ref, acc_ref, r_ref):
    tq, kb = SB_TQ, SB_KB
    i = pl.program_id(2)
    lane = lax.broadcasted_iota(jnp.int32, (tq, LANES), 1)
    tri = tri_ref[...]
    lane_blocks = range(SB_LANE_BLOCKS)

    qs = []
    for lb in lane_blocks:
        qf = q_ref[:, lb * LANES:(lb + 1) * LANES].astype(F32) * (SB_HEAD_DIM ** -0.5)
        per_head = []
        for h2 in range(SB_HEADS_PER_LANE_BLOCK):
            in_head = (lane >= h2 * SB_HEAD_DIM) & (lane < (h2 + 1) * SB_HEAD_DIM)
            per_head.append(jnp.where(in_head, qf, 0.0).astype(BF16))
        qs.append(jnp.concatenate(per_head, axis=0))

    def keys(ref, kstart, n_keys):
        return [ref[pl.ds(kstart, n_keys), lb * LANES:(lb + 1) * LANES] for lb in lane_blocks]

    def diagonal_blocks(with_previous):
        row = lax.broadcasted_iota(jnp.int32, (SB_ROWS, kb), 0) & (tq - 1)
        col = lax.broadcasted_iota(jnp.int32, (SB_ROWS, kb), 1)
        diag = pl.multiple_of(i * kb, kb)
        if with_previous:
            prev = pl.multiple_of((i - 1) * kb, kb)
            zs = [_sb_scores(q, k2) for q, k2 in zip(qs, keys(k_ref, prev, 2 * kb))]
            zs_prev = [z[:, :kb] for z in zs]
            zs = [z[:, kb:] for z in zs]
        else:
            zs = [_sb_scores(q, k) for q, k in zip(qs, keys(k_ref, diag, kb))]
        pvs, rs = _sb_key_block(zs, keys(v_ref, diag, kb), tri, None, col < row)
        if with_previous:
            pvs2, rs = _sb_key_block(zs_prev, keys(v_ref, prev, kb), tri, rs, None)
            pvs = [pv + pv2 for pv, pv2 in zip(pvs, pvs2)]
        for lb in lane_blocks:
            acc_ref[lb] = pvs[lb]
            r_ref[lb] = rs[lb]

    @pl.when(i == 0)
    def _():
        diagonal_blocks(False)

    @pl.when(i > 0)
    def _():
        diagonal_blocks(True)

    def min_sum():
        return jnp.min(r_ref[...])

    def cond(carry):
        j, m = carry
        return jnp.logical_and(j >= 0, m < SB_DONE_SUM)

    def body(carry):
        j, _ = carry
        kstart = pl.multiple_of(j * kb, kb)
        zs = [_sb_scores(q, k) for q, k in zip(qs, keys(k_ref, kstart, kb))]
        pvs, rs = _sb_key_block(zs, keys(v_ref, kstart, kb), tri,
                                [r_ref[lb] for lb in lane_blocks], None)
        for lb in lane_blocks:
            acc_ref[lb] += pvs[lb]
            r_ref[lb] = rs[lb]
        return j - 1, min_sum()

    lax.while_loop(cond, body, (i - 2, min_sum()))

    for lb in lane_blocks:
        acc = acc_ref[lb]
        o_ref[:, lb * LANES:(lb + 1) * LANES] = jnp.where(
            lane < SB_HEAD_DIM, acc[:tq], acc[tq:]).astype(o_ref.dtype)


def _sb_attention(qkv, batch, seq):
    t = qkv.shape[0]
    nq = seq // SB_TQ
    width = SB_LANE_BLOCKS * LANES
    groups = D_MODEL // width
    j = jnp.arange(SB_KB)
    tri = (j[:, None] > j[None, :]).astype(BF16)
    return pl.pallas_call(
        _sb_attn_kernel,
        grid=(batch, groups, nq),
        in_specs=[
            pl.BlockSpec((SB_TQ, width), lambda b, p, i: (b * nq + i, p)),
            pl.BlockSpec((seq, width), lambda b, p, i: (b, groups + p)),
            pl.BlockSpec((seq, width), lambda b, p, i: (b, 2 * groups + p)),
            _resident((SB_KB, SB_KB), lambda b, p, i: (0, 0)),
        ],
        out_specs=pl.BlockSpec((SB_TQ, width), lambda b, p, i: (b * nq + i, p)),
        out_shape=jax.ShapeDtypeStruct((t, D_MODEL), BF16),
        scratch_shapes=[
            pltpu.VMEM((SB_LANE_BLOCKS, SB_ROWS, LANES), F32),
            pltpu.VMEM((SB_LANE_BLOCKS, SB_ROWS, LANES), F32),
        ],
        compiler_params=_params("parallel", "parallel", "arbitrary"),
        name="sb_attention",
    )(qkv, qkv, qkv, tri)


def _retention_kernel(q_ref, k_ref, kd_ref, v_ref, sg_ref, din_ref, qd_ref, cd_ref,
                      o_ref, state_ref):
    n = pl.program_id(2)

    @pl.when(n == 0)
    def _():
        state_ref[...] = jnp.zeros_like(state_ref)

    for hh in range(RET_HEADS_PER_STEP):
        qk_cols = slice(hh * RET_QK_DIM, (hh + 1) * RET_QK_DIM)
        v_cols = slice(hh * RET_V_DIM, (hh + 1) * RET_V_DIM)
        qb = q_ref[:, qk_cols]
        vb = v_ref[:, v_cols]

        scores = lax.dot_general(qb, k_ref[:, qk_cols], (((1,), (1,)), ((), ())),
                                 preferred_element_type=F32) * din_ref[hh]
        inner = jnp.dot(scores.astype(BF16), vb, preferred_element_type=F32)
        state = state_ref[hh]
        cross = jnp.dot(qb, state.astype(BF16), preferred_element_type=F32) * qd_ref[hh]
        state_ref[hh] = state * cd_ref[hh] + lax.dot_general(
            kd_ref[:, qk_cols], vb, (((0,), (0,)), ((), ())), preferred_element_type=F32)

        o = inner + cross
        mu = jnp.mean(o, axis=-1, keepdims=True)
        xc = o - mu
        var = jnp.mean(xc * xc, axis=-1, keepdims=True)
        o = xc * lax.rsqrt(var + GN_EPS)
        o_ref[:, v_cols] = (sg_ref[:, v_cols].astype(F32) * o).astype(o_ref.dtype)


def _retention(qk, v, sg, decay_in, qd, cd, batch, seq):
    t = qk.shape[0]
    c = RET_BLOCK
    nc = seq // c
    hs = RET_HEADS_PER_STEP
    groups = RET_HEADS // hs
    qk_w, v_w = hs * RET_QK_DIM, hs * RET_V_DIM
    return pl.pallas_call(
        _retention_kernel,
        grid=(batch, groups, nc),
        in_specs=[
            pl.BlockSpec((c, qk_w), lambda b, h, n: (b * nc + n, h)),
            pl.BlockSpec((c, qk_w), lambda b, h, n: (b * nc + n, groups + h)),
            pl.BlockSpec((c, qk_w), lambda b, h, n: (b * nc + n, 2 * groups + h)),
            pl.BlockSpec((c, v_w), lambda b, h, n: (b * nc + n, h)),
            pl.BlockSpec((c, v_w), lambda b, h, n: (b * nc + n, h)),
            pl.BlockSpec((hs, c, c), lambda b, h, n: (h, 0, 0)),
            pl.BlockSpec((hs, c, RET_V_DIM), lambda b, h, n: (h, 0, 0)),
            pl.BlockSpec((hs, 1, RET_V_DIM), lambda b, h, n: (h, 0, 0)),
        ],
        out_specs=pl.BlockSpec((c, v_w), lambda b, h, n: (b * nc + n, h)),
        out_shape=jax.ShapeDtypeStruct((t, RET_HEADS * RET_V_DIM), BF16),
        scratch_shapes=[pltpu.VMEM((hs, RET_QK_DIM, RET_V_DIM), F32)],
        compiler_params=_params("parallel", "parallel", "arbitrary"),
        name="retention",
    )(qk, qk, qk, v, sg, decay_in, qd, cd)


def kernel(x, w_sb_in, w_sb_out, w_ret_in, w_ret_out, g_mix, g_mlp, w_mlp_up,
           w_mlp_down, g_final):
    batch, seq, d = x.shape
    depth = g_mix.shape[0]
    w_sb_in, w_sb_out, w_ret_in, w_ret_out, w_mlp_up, w_mlp_down = (
        _cast_bf16(w) for w in (w_sb_in, w_sb_out, w_ret_in, w_ret_out,
                                w_mlp_up, w_mlp_down))
    cos, sin, decay_in, qd, kd, cd = _retention_constants(seq)
    h = x.reshape(batch * seq, d)
    for layer in range(depth):
        i = layer // N_MIXERS
        if layer % N_MIXERS == 0:
            qkv = _sb_proj(h, g_mix[layer], w_sb_in, i)
            mixed = _sb_attention(qkv, batch, seq)
            w_out = w_sb_out
        else:
            qk, v, sg = _ret_proj(h, g_mix[layer], w_ret_in, i, cos, sin, kd, seq)
            mixed = _retention(qk, v, sg, decay_in, qd, cd, batch, seq)
            w_out = w_ret_out
        h = _out_proj_mlp(mixed, w_out, i, h, g_mlp[layer], w_mlp_up, w_mlp_down,
                          layer, g_final, layer == depth - 1)
    return h.reshape(batch, seq, d)
```

```python
import functools

import jax
import jax.numpy as jnp
from jax import lax
from jax.experimental import pallas as pl
from jax.experimental.pallas import tpu as pltpu

F32 = jnp.float32
BF16 = jnp.bfloat16

D_MODEL = 1024
N_MIXERS = 2
SB_HEADS = 16
SB_HEAD_DIM = D_MODEL // SB_HEADS
RET_HEADS = 4
RET_QK_DIM = D_MODEL // RET_HEADS
RET_V_DIM = 2 * D_MODEL // RET_HEADS
ROPE_BASE = 10000.0
MLP_HIDDEN = 4 * D_MODEL
RMS_EPS = 1e-6
GN_EPS = 1e-6
LOG2_E = 1.4426950408889634

LANES = 128
VMEM_LIMIT = 56 * 1024 * 1024

ROW_TILE = 512
CAST_BLOCK_ELEMS = 1 << 20
SB_TQ = 256
SB_KB = 256
SB_LANE_BLOCKS = 4
RET_BLOCK = 256
RET_HEADS_PER_STEP = 4


def _rms_norm_f32(x, g):
    ms = jnp.mean(x * x, axis=-1, keepdims=True)
    return x * lax.rsqrt(ms + RMS_EPS) * g


def _resident(block_shape, index_map):
    return pl.BlockSpec(block_shape, index_map, pipeline_mode=pl.Buffered(1))


def _params(*semantics):
    return pltpu.CompilerParams(dimension_semantics=semantics,
                                vmem_limit_bytes=VMEM_LIMIT)


def _cast_kernel(x_ref, o_ref):
    o_ref[...] = x_ref[...].astype(o_ref.dtype)


def _cast_bf16(w):
    layers, k, n = w.shape
    rows = min(k, 1 << ((CAST_BLOCK_ELEMS // n).bit_length() - 1))
    spec = pl.BlockSpec((None, rows, n), lambda a, r: (a, r, 0))
    return pl.pallas_call(
        _cast_kernel,
        grid=(layers, k // rows),
        in_specs=[spec],
        out_specs=spec,
        out_shape=jax.ShapeDtypeStruct(w.shape, BF16),
        compiler_params=_params("parallel", "parallel"),
        name="cast_bf16",
    )(w)


def _sb_proj_kernel(x_ref, g_ref, w_ref, o_ref):
    xn = _rms_norm_f32(x_ref[...], g_ref[...]).astype(BF16)
    for j in range(w_ref.shape[1] // D_MODEL):
        cols = slice(j * D_MODEL, (j + 1) * D_MODEL)
        o_ref[:, cols] = jnp.dot(
            xn, w_ref[:, cols], preferred_element_type=F32).astype(o_ref.dtype)


def _sb_proj(x, g, w, layer):
    t, d = x.shape
    n = w.shape[2]
    return pl.pallas_call(
        _sb_proj_kernel,
        grid=(t // ROW_TILE,),
        in_specs=[
            pl.BlockSpec((ROW_TILE, d), lambda i: (i, 0)),
            _resident((1, d), lambda i: (0, 0)),
            _resident((None, d, n), lambda i: (layer, 0, 0)),
        ],
        out_specs=pl.BlockSpec((ROW_TILE, n), lambda i: (i, 0)),
        out_shape=jax.ShapeDtypeStruct((t, n), BF16),
        compiler_params=_params("parallel"),
        name="sb_proj",
    )(x, g.reshape(1, d), w)


def _rotate(x, cos, sin):
    half = x.shape[-1] // 2
    x1, x2 = x[:, :half], x[:, half:]
    return jnp.concatenate([x1 * cos - x2 * sin, x1 * sin + x2 * cos], axis=1)


def _ret_proj_kernel(x_ref, g_ref, w_ref, cos_ref, sin_ref, kd_ref,
                     qk_ref, v_ref, sg_ref):
    xn = _rms_norm_f32(x_ref[...], g_ref[...]).astype(BF16)
    cos, sin = cos_ref[...], sin_ref[...]

    def proj(j):
        return jnp.dot(xn, w_ref[:, j * D_MODEL:(j + 1) * D_MODEL],
                       preferred_element_type=F32)

    q = proj(0)
    for hh in range(RET_HEADS):
        cols = slice(hh * RET_QK_DIM, (hh + 1) * RET_QK_DIM)
        qk_ref[:, cols] = _rotate(q[:, cols], cos, sin).astype(BF16)
    k = proj(1)
    for hh in range(RET_HEADS):
        cols = slice(hh * RET_QK_DIM, (hh + 1) * RET_QK_DIM)
        kr = _rotate(k[:, cols], cos, sin) * (RET_QK_DIM ** -0.5)
        qk_ref[:, D_MODEL + hh * RET_QK_DIM:D_MODEL + (hh + 1) * RET_QK_DIM] = kr.astype(BF16)
        qk_ref[:, 2 * D_MODEL + hh * RET_QK_DIM:2 * D_MODEL + (hh + 1) * RET_QK_DIM] = (
            kr * kd_ref[hh]).astype(BF16)
    for j in range(2):
        cols = slice(j * D_MODEL, (j + 1) * D_MODEL)
        v_ref[:, cols] = proj(2 + j).astype(BF16)
        gate = proj(4 + j)
        sg_ref[:, cols] = (gate * (1.0 / (1.0 + jnp.exp(-gate)))).astype(BF16)


def _retention_constants(seq):
    c = RET_BLOCK
    half = RET_QK_DIM // 2
    inv_freq = 1.0 / (ROPE_BASE ** jnp.linspace(0.0, 1.0, half, dtype=F32))
    ang = jnp.arange(seq).astype(F32)[:, None] * inv_freq[None, :]
    cos, sin = jnp.cos(ang), jnp.sin(ang)

    log_gamma = jnp.log1p(-jnp.exp2(-5.0 - jnp.arange(RET_HEADS, dtype=F32)))
    idx = jnp.arange(c, dtype=F32)
    rel = idx[:, None] - idx[None, :]
    decay_in = jnp.where(
        rel >= 0, jnp.exp(log_gamma[:, None, None] * jnp.maximum(rel, 0.0)), 0.0)
    q_decay = jnp.exp(log_gamma[:, None] * (idx + 1.0))
    k_decay = jnp.exp(log_gamma[:, None] * (c - 1.0 - idx))
    chunk_decay = jnp.exp(log_gamma * c)
    qd = jnp.broadcast_to(q_decay[:, :, None], (RET_HEADS, c, RET_V_DIM))
    k_decay_tile = jnp.tile(k_decay, (1, ROW_TILE // c))
    kd = jnp.broadcast_to(k_decay_tile[:, :, None], (RET_HEADS, ROW_TILE, RET_QK_DIM))
    cd = jnp.broadcast_to(chunk_decay[:, None, None], (RET_HEADS, 1, RET_V_DIM))
    return cos, sin, decay_in, qd, kd, cd


def _ret_proj(x, g, w, layer, cos, sin, kd, seq):
    t, d = x.shape
    n = w.shape[2]
    half = RET_QK_DIM // 2
    pos_blocks = seq // ROW_TILE
    row = lambda i: (i, 0)
    return pl.pallas_call(
        _ret_proj_kernel,
        grid=(t // ROW_TILE,),
        in_specs=[
            pl.BlockSpec((ROW_TILE, d), row),
            _resident((1, d), lambda i: (0, 0)),
            _resident((None, d, n), lambda i: (layer, 0, 0)),
            pl.BlockSpec((ROW_TILE, half), lambda i: (i % pos_blocks, 0)),
            pl.BlockSpec((ROW_TILE, half), lambda i: (i % pos_blocks, 0)),
            _resident((RET_HEADS, ROW_TILE, RET_QK_DIM), lambda i: (0, 0, 0)),
        ],
        out_specs=[
            pl.BlockSpec((ROW_TILE, 3 * d), row),
            pl.BlockSpec((ROW_TILE, 2 * d), row),
            pl.BlockSpec((ROW_TILE, 2 * d), row),
        ],
        out_shape=[
            jax.ShapeDtypeStruct((t, 3 * d), BF16),
            jax.ShapeDtypeStruct((t, 2 * d), BF16),
            jax.ShapeDtypeStruct((t, 2 * d), BF16),
        ],
        compiler_params=_params("parallel"),
        name="ret_proj",
    )(x, g.reshape(1, d), w, cos, sin, kd)


def _mlp_kernel(a_ref, wo_ref, r_ref, g_ref, wu_ref, wd_ref, go_ref, o_ref, *,
                h_chunk, final_norm):
    x = r_ref[...] + jnp.dot(a_ref[...], wo_ref[...], preferred_element_type=F32)
    xn = _rms_norm_f32(x, g_ref[...]).astype(BF16)
    y = x
    for j in range(wu_ref.shape[1] // h_chunk):
        cols = slice(j * h_chunk, (j + 1) * h_chunk)
        u = jnp.maximum(jnp.dot(xn, wu_ref[:, cols], preferred_element_type=F32), 0.0)
        y = y + jnp.dot((u * u).astype(BF16), wd_ref[cols, :],
                        preferred_element_type=F32)
    if final_norm:
        y = _rms_norm_f32(y, go_ref[...])
    o_ref[...] = y


def _out_proj_mlp(a, w_out, out_layer, res, g, w_up, w_down, layer, g_out, final_norm):
    t, d = res.shape
    k = a.shape[1]
    hid = w_up.shape[2]
    row = lambda i: (i, 0)
    return pl.pallas_call(
        functools.partial(_mlp_kernel, h_chunk=D_MODEL, final_norm=final_norm),
        grid=(t // ROW_TILE,),
        in_specs=[
            pl.BlockSpec((ROW_TILE, k), row),
            _resident((None, k, d), lambda i: (out_layer, 0, 0)),
            pl.BlockSpec((ROW_TILE, d), row),
            _resident((1, d), lambda i: (0, 0)),
            _resident((None, d, hid), lambda i: (layer, 0, 0)),
            _resident((None, hid, d), lambda i: (layer, 0, 0)),
            _resident((1, d), lambda i: (0, 0)),
        ],
        out_specs=pl.BlockSpec((ROW_TILE, d), row),
        out_shape=jax.ShapeDtypeStruct((t, d), F32),
        compiler_params=_params("parallel"),
        name="out_proj_mlp",
    )(a, w_out, res, g.reshape(1, d), w_up, w_down, g_out.reshape(1, d))


SB_DONE_SUM = 104.0
SB_HEADS_PER_LANE_BLOCK = LANES // SB_HEAD_DIM


SB_ROWS = SB_HEADS_PER_LANE_BLOCK * SB_TQ


def _sb_scores(q, keys):
    return lax.dot_general(q, keys, (((1,), (1,)), ((), ())),
                           preferred_element_type=F32)


def _sb_key_block(zs, vblks, tri, rs, mask):
    sps, log_sigs = [], []
    for z in zs:
        log1pe = jnp.log(1.0 + jnp.exp2(jnp.abs(z) * (-LOG2_E)))
        sp = jnp.maximum(z, 0.0) + log1pe
        log_sigs.append(z - sp)
        sps.append(sp if mask is None else jnp.where(mask, sp, 0.0))
    c_all = jnp.dot(jnp.concatenate([sp.astype(BF16) for sp in sps], axis=0), tri,
                    preferred_element_type=F32)
    pvs, rs_new = [], []
    for lb, (sp, log_sig, vblk) in enumerate(zip(sps, log_sigs, vblks)):
        c = c_all[lb * SB_ROWS:(lb + 1) * SB_ROWS]
        if rs is not None:
            c = c + jnp.concatenate([rs[lb]] * (SB_KB // LANES), axis=1)
        a = jnp.exp(log_sig - c)
        if mask is not None:
            a = jnp.where(mask, a, 0.0)
        pvs.append(jnp.dot(a.astype(BF16), vblk, preferred_element_type=F32))
        rs_new.append(jnp.broadcast_to(c[:, 0:1] + sp[:, 0:1], (SB_ROWS, LANES)))
    return pvs, rs_new


def _sb_attn_kernel(q_ref, k_ref, v_ref, tri_ref, o_ref, acc_ref, r_ref):
    tq, kb = SB_TQ, SB_KB
    i = pl.program_id(2)
    lane = lax.broadcasted_iota(jnp.int32, (tq, LANES), 1)
    tri = tri_ref[...]
    lane_blocks = range(SB_LANE_BLOCKS)

    qs = []
    for lb in lane_blocks:
        qf = q_ref[:, lb * LANES:(lb + 1) * LANES].astype(F32) * (SB_HEAD_DIM ** -0.5)
        per_head = []
        for h2 in range(SB_HEADS_PER_LANE_BLOCK):
            in_head = (lane >= h2 * SB_HEAD_DIM) & (lane < (h2 + 1) * SB_HEAD_DIM)
            per_head.append(jnp.where(in_head, qf, 0.0).astype(BF16))
        qs.append(jnp.concatenate(per_head, axis=0))

    def keys(ref, kstart, n_keys):
        return [ref[pl.ds(kstart, n_keys), lb * LANES:(lb + 1) * LANES] for lb in lane_blocks]

    def diagonal_blocks(with_previous):
        row = lax.broadcasted_iota(jnp.int32, (SB_ROWS, kb), 0) & (tq - 1)
        col = lax.broadcasted_iota(jnp.int32, (SB_ROWS, kb), 1)
        diag = pl.multiple_of(i * kb, kb)
        if with_previous:
            prev = pl.multiple_of((i - 1) * kb, kb)
            zs = [_sb_scores(q, k2) for q, k2 in zip(qs, keys(k_ref, prev, 2 * kb))]
            zs_prev = [z[:, :kb] for z in zs]
            zs = [z[:, kb:] for z in zs]
        else:
            zs = [_sb_scores(q, k) for q, k in zip(qs, keys(k_ref, diag, kb))]
        pvs, rs = _sb_key_block(zs, keys(v_ref, diag, kb), tri, None, col < row)
        if with_previous:
            pvs2, rs = _sb_key_block(zs_prev, keys(v_ref, prev, kb), tri, rs, None)
            pvs = [pv + pv2 for pv, pv2 in zip(pvs, pvs2)]
        for lb in lane_blocks:
            acc_ref[lb] = pvs[lb]
            r_ref[lb] = rs[lb]

    @pl.when(i == 0)
    def _():
        diagonal_blocks(False)

    @pl.when(i > 0)
    def _():
        diagonal_blocks(True)

    def min_sum():
        return jnp.min(r_ref[...])

    def cond(carry):
        j, m = carry
        return jnp.logical_and(j >= 0, m < SB_DONE_SUM)

    def body(carry):
        j, _ = carry
        kstart = pl.multiple_of(j * kb, kb)
        zs = [_sb_scores(q, k) for q, k in zip(qs, keys(k_ref, kstart, kb))]
        pvs, rs = _sb_key_block(zs, keys(v_ref, kstart, kb), tri,
                                [r_ref[lb] for lb in lane_blocks], None)
        for lb in lane_blocks:
            acc_ref[lb] += pvs[lb]
            r_ref[lb] = rs[lb]
        return j - 1, min_sum()

    lax.while_loop(cond, body, (i - 2, min_sum()))

    for lb in lane_blocks:
        acc = acc_ref[lb]
        o_ref[:, lb * LANES:(lb + 1) * LANES] = jnp.where(
            lane < SB_HEAD_DIM, acc[:tq], acc[tq:]).astype(o_ref.dtype)


def _sb_attention(qkv, batch, seq):
    t = qkv.shape[0]
    nq = seq // SB_TQ
    width = SB_LANE_BLOCKS * LANES
    groups = D_MODEL // width
    j = jnp.arange(SB_KB)
    tri = (j[:, None] > j[None, :]).astype(BF16)
    return pl.pallas_call(
        _sb_attn_kernel,
        grid=(batch, groups, nq),
        in_specs=[
            pl.BlockSpec((SB_TQ, width), lambda b, p, i: (b * nq + i, p)),
            pl.BlockSpec((seq, width), lambda b, p, i: (b, groups + p)),
            pl.BlockSpec((seq, width), lambda b, p, i: (b, 2 * groups + p)),
            _resident((SB_KB, SB_KB), lambda b, p, i: (0, 0)),
        ],
        out_specs=pl.BlockSpec((SB_TQ, width), lambda b, p, i: (b * nq + i, p)),
        out_shape=jax.ShapeDtypeStruct((t, D_MODEL), BF16),
        scratch_shapes=[
            pltpu.VMEM((SB_LANE_BLOCKS, SB_ROWS, LANES), F32),
            pltpu.VMEM((SB_LANE_BLOCKS, SB_ROWS, LANES), F32),
        ],
        compiler_params=_params("parallel", "parallel", "arbitrary"),
        name="sb_attention",
    )(qkv, qkv, qkv, tri)


def _retention_kernel(q_ref, k_ref, kd_ref, v_ref, sg_ref, din_ref, qd_ref, cd_ref,
                      o_ref, state_ref):
    n = pl.program_id(2)

    @pl.when(n == 0)
    def _():
        state_ref[...] = jnp.zeros_like(state_ref)

    for hh in range(RET_HEADS_PER_STEP):
        qk_cols = slice(hh * RET_QK_DIM, (hh + 1) * RET_QK_DIM)
        v_cols = slice(hh * RET_V_DIM, (hh + 1) * RET_V_DIM)
        qb = q_ref[:, qk_cols]
        vb = v_ref[:, v_cols]

        scores = lax.dot_general(qb, k_ref[:, qk_cols], (((1,), (1,)), ((), ())),
                                 preferred_element_type=F32) * din_ref[hh]
        inner = jnp.dot(scores.astype(BF16), vb, preferred_element_type=F32)
        state = state_ref[hh]
        cross = jnp.dot(qb, state.astype(BF16), preferred_element_type=F32) * qd_ref[hh]
        state_ref[hh] = state * cd_ref[hh] + lax.dot_general(
            kd_ref[:, qk_cols], vb, (((0,), (0,)), ((), ())), preferred_element_type=F32)

        o = inner + cross
        mu = jnp.mean(o, axis=-1, keepdims=True)
        xc = o - mu
        var = jnp.mean(xc * xc, axis=-1, keepdims=True)
        o = xc * lax.rsqrt(var + GN_EPS)
        o_ref[:, v_cols] = (sg_ref[:, v_cols].astype(F32) * o).astype(o_ref.dtype)


def _retention(qk, v, sg, decay_in, qd, cd, batch, seq):
    t = qk.shape[0]
    c = RET_BLOCK
    nc = seq // c
    hs = RET_HEADS_PER_STEP
    groups = RET_HEADS // hs
    qk_w, v_w = hs * RET_QK_DIM, hs * RET_V_DIM
    return pl.pallas_call(
        _retention_kernel,
        grid=(batch, groups, nc),
        in_specs=[
            pl.BlockSpec((c, qk_w), lambda b, h, n: (b * nc + n, h)),
            pl.BlockSpec((c, qk_w), lambda b, h, n: (b * nc + n, groups + h)),
            pl.BlockSpec((c, qk_w), lambda b, h, n: (b * nc + n, 2 * groups + h)),
            pl.BlockSpec((c, v_w), lambda b, h, n: (b * nc + n, h)),
            pl.BlockSpec((c, v_w), lambda b, h, n: (b * nc + n, h)),
            pl.BlockSpec((hs, c, c), lambda b, h, n: (h, 0, 0)),
            pl.BlockSpec((hs, c, RET_V_DIM), lambda b, h, n: (h, 0, 0)),
            pl.BlockSpec((hs, 1, RET_V_DIM), lambda b, h, n: (h, 0, 0)),
        ],
        out_specs=pl.BlockSpec((c, v_w), lambda b, h, n: (b * nc + n, h)),
        out_shape=jax.ShapeDtypeStruct((t, RET_HEADS * RET_V_DIM), BF16),
        scratch_shapes=[pltpu.VMEM((hs, RET_QK_DIM, RET_V_DIM), F32)],
        compiler_params=_params("parallel", "parallel", "arbitrary"),
        name="retention",
    )(qk, qk, qk, v, sg, decay_in, qd, cd)


def kernel(x, w_sb_in, w_sb_out, w_ret_in, w_ret_out, g_mix, g_mlp, w_mlp_up,
           w_mlp_down, g_final):
    batch, seq, d = x.shape
    depth = g_mix.shape[0]
    w_sb_in, w_sb_out, w_ret_in, w_ret_out, w_mlp_up, w_mlp_down = (
        _cast_bf16(w) for w in (w_sb_in, w_sb_out, w_ret_in, w_ret_out,
                                w_mlp_up, w_mlp_down))
    cos, sin, decay_in, qd, kd, cd = _retention_constants(seq)
    h = x.reshape(batch * seq, d)
    for layer in range(depth):
        i = layer // N_MIXERS
        if layer % N_MIXERS == 0:
            qkv = _sb_proj(h, g_mix[layer], w_sb_in, i)
            mixed = _sb_attention(qkv, batch, seq)
            w_out = w_sb_out
        else:
            qk, v, sg = _ret_proj(h, g_mix[layer], w_ret_in, i, cos, sin, kd, seq)
            mixed = _retention(qk, v, sg, decay_in, qd, cd, batch, seq)
            w_out = w_ret_out
        h = _out_proj_mlp(mixed, w_out, i, h, g_mlp[layer], w_mlp_up, w_mlp_down,
                          layer, g_final, layer == depth - 1)
    return h.reshape(batch, seq, d)
```

```python
import functools

import jax
import jax.numpy as jnp
from jax import lax
from jax.experimental import pallas as pl
from jax.experimental.pallas import tpu as pltpu

F32 = jnp.float32
BF16 = jnp.bfloat16

D_MODEL = 1024
N_MIXERS = 2
SB_HEADS = 16
SB_HEAD_DIM = D_MODEL // SB_HEADS
RET_HEADS = 4
RET_QK_DIM = D_MODEL // RET_HEADS
RET_V_DIM = 2 * D_MODEL // RET_HEADS
ROPE_BASE = 10000.0
MLP_HIDDEN = 4 * D_MODEL
RMS_EPS = 1e-6
GN_EPS = 1e-6
LOG2_E = 1.4426950408889634

LANES = 128
VMEM_LIMIT = 56 * 1024 * 1024

ROW_TILE = 512
CAST_BLOCK_ELEMS = 1 << 20
SB_TQ = 256
SB_KB = 256
SB_LANE_BLOCKS = 4
RET_BLOCK = 256
RET_HEADS_PER_STEP = 4


def _rms_norm_f32(x, g):
    ms = jnp.mean(x * x, axis=-1, keepdims=True)
    return x * lax.rsqrt(ms + RMS_EPS) * g


def _resident(block_shape, index_map):
    return pl.BlockSpec(block_shape, index_map, pipeline_mode=pl.Buffered(1))


def _params(*semantics):
    return pltpu.CompilerParams(dimension_semantics=semantics,
                                vmem_limit_bytes=VMEM_LIMIT)


def _cast_kernel(x_ref, o_ref):
    o_ref[...] = x_ref[...].astype(o_ref.dtype)


def _cast_bf16(w):
    layers, k, n = w.shape
    rows = min(k, 1 << ((CAST_BLOCK_ELEMS // n).bit_length() - 1))
    spec = pl.BlockSpec((None, rows, n), lambda a, r: (a, r, 0))
    return pl.pallas_call(
        _cast_kernel,
        grid=(layers, k // rows),
        in_specs=[spec],
        out_specs=spec,
        out_shape=jax.ShapeDtypeStruct(w.shape, BF16),
        compiler_params=_params("parallel", "parallel"),
        name="cast_bf16",
    )(w)


def _sb_proj_kernel(x_ref, g_ref, w_ref, o_ref):
    xn = _rms_norm_f32(x_ref[...], g_ref[...]).astype(BF16)
    for j in range(w_ref.shape[1] // D_MODEL):
        cols = slice(j * D_MODEL, (j + 1) * D_MODEL)
        o_ref[:, cols] = jnp.dot(
            xn, w_ref[:, cols], preferred_element_type=F32).astype(o_ref.dtype)


def _sb_proj(x, g, w, layer):
    t, d = x.shape
    n = w.shape[2]
    return pl.pallas_call(
        _sb_proj_kernel,
        grid=(t // ROW_TILE,),
        in_specs=[
            pl.BlockSpec((ROW_TILE, d), lambda i: (i, 0)),
            _resident((1, d), lambda i: (0, 0)),
            _resident((None, d, n), lambda i: (layer, 0, 0)),
        ],
        out_specs=pl.BlockSpec((ROW_TILE, n), lambda i: (i, 0)),
        out_shape=jax.ShapeDtypeStruct((t, n), BF16),
        compiler_params=_params("parallel"),
        name="sb_proj",
    )(x, g.reshape(1, d), w)


def _rotate(x, cos, sin):
    half = x.shape[-1] // 2
    x1, x2 = x[:, :half], x[:, half:]
    return jnp.concatenate([x1 * cos - x2 * sin, x1 * sin + x2 * cos], axis=1)


def _ret_proj_kernel(x_ref, g_ref, w_ref, cos_ref, sin_ref, kd_ref,
                     qk_ref, v_ref, sg_ref):
    xn = _rms_norm_f32(x_ref[...], g_ref[...]).astype(BF16)
    cos, sin = cos_ref[...], sin_ref[...]

    def proj(j):
        return jnp.dot(xn, w_ref[:, j * D_MODEL:(j + 1) * D_MODEL],
                       preferred_element_type=F32)

    q = proj(0)
    for hh in range(RET_HEADS):
        cols = slice(hh * RET_QK_DIM, (hh + 1) * RET_QK_DIM)
        qk_ref[:, cols] = _rotate(q[:, cols], cos, sin).astype(BF16)
    k = proj(1)
    for hh in range(RET_HEADS):
        cols = slice(hh * RET_QK_DIM, (hh + 1) * RET_QK_DIM)
        kr = _rotate(k[:, cols], cos, sin) * (RET_QK_DIM ** -0.5)
        qk_ref[:, D_MODEL + hh * RET_QK_DIM:D_MODEL + (hh + 1) * RET_QK_DIM] = kr.astype(BF16)
        qk_ref[:, 2 * D_MODEL + hh * RET_QK_DIM:2 * D_MODEL + (hh + 1) * RET_QK_DIM] = (
            kr * kd_ref[hh]).astype(BF16)
    for j in range(2):
        cols = slice(j * D_MODEL, (j + 1) * D_MODEL)
        v_ref[:, cols] = proj(2 + j).astype(BF16)
        gate = proj(4 + j)
        sg_ref[:, cols] = (gate * (1.0 / (1.0 + jnp.exp(-gate)))).astype(BF16)


def _retention_constants(seq):
    c = RET_BLOCK
    half = RET_QK_DIM // 2
    inv_freq = 1.0 / (ROPE_BASE ** jnp.linspace(0.0, 1.0, half, dtype=F32))
    ang = jnp.arange(seq).astype(F32)[:, None] * inv_freq[None, :]
    cos, sin = jnp.cos(ang), jnp.sin(ang)

    log_gamma = jnp.log1p(-jnp.exp2(-5.0 - jnp.arange(RET_HEADS, dtype=F32)))
    idx = jnp.arange(c, dtype=F32)
    rel = idx[:, None] - idx[None, :]
    decay_in = jnp.where(
        rel >= 0, jnp.exp(log_gamma[:, None, None] * jnp.maximum(rel, 0.0)), 0.0)
    q_decay = jnp.exp(log_gamma[:, None] * (idx + 1.0))
    k_decay = jnp.exp(log_gamma[:, None] * (c - 1.0 - idx))
    chunk_decay = jnp.exp(log_gamma * c)
    qd = jnp.broadcast_to(q_decay[:, :, None], (RET_HEADS, c, RET_V_DIM))
    k_decay_tile = jnp.tile(k_decay, (1, ROW_TILE // c))
    kd = jnp.broadcast_to(k_decay_tile[:, :, None], (RET_HEADS, ROW_TILE, RET_QK_DIM))
    cd = jnp.broadcast_to(chunk_decay[:, None, None], (RET_HEADS, 1, RET_V_DIM))
    return cos, sin, decay_in, qd, kd, cd


def _ret_proj(x, g, w, layer, cos, sin, kd, seq):
    t, d = x.shape
    n = w.shape[2]
    half = RET_QK_DIM // 2
    pos_blocks = seq // ROW_TILE
    row = lambda i: (i, 0)
    return pl.pallas_call(
        _ret_proj_kernel,
        grid=(t // ROW_TILE,),
        in_specs=[
            pl.BlockSpec((ROW_TILE, d), row),
            _resident((1, d), lambda i: (0, 0)),
            _resident((None, d, n), lambda i: (layer, 0, 0)),
            pl.BlockSpec((ROW_TILE, half), lambda i: (i % pos_blocks, 0)),
            pl.BlockSpec((ROW_TILE, half), lambda i: (i % pos_blocks, 0)),
            _resident((RET_HEADS, ROW_TILE, RET_QK_DIM), lambda i: (0, 0, 0)),
        ],
        out_specs=[
            pl.BlockSpec((ROW_TILE, 3 * d), row),
            pl.BlockSpec((ROW_TILE, 2 * d), row),
            pl.BlockSpec((ROW_TILE, 2 * d), row),
        ],
        out_shape=[
            jax.ShapeDtypeStruct((t, 3 * d), BF16),
            jax.ShapeDtypeStruct((t, 2 * d), BF16),
            jax.ShapeDtypeStruct((t, 2 * d), BF16),
        ],
        compiler_params=_params("parallel"),
        name="ret_proj",
    )(x, g.reshape(1, d), w, cos, sin, kd)


def _mlp_kernel(a_ref, wo_ref, r_ref, g_ref, wu_ref, wd_ref, go_ref, o_ref, *,
                h_chunk, final_norm):
    x = r_ref[...] + jnp.dot(a_ref[...], wo_ref[...], preferred_element_type=F32)
    xn = _rms_norm_f32(x, g_ref[...]).astype(BF16)
    y = x
    for j in range(wu_ref.shape[1] // h_chunk):
        cols = slice(j * h_chunk, (j + 1) * h_chunk)
        u = jnp.maximum(jnp.dot(xn, wu_ref[:, cols], preferred_element_type=F32), 0.0)
        y = y + jnp.dot((u * u).astype(BF16), wd_ref[cols, :],
                        preferred_element_type=F32)
    if final_norm:
        y = _rms_norm_f32(y, go_ref[...])
    o_ref[...] = y


def _out_proj_mlp(a, w_out, out_layer, res, g, w_up, w_down, layer, g_out, final_norm):
    t, d = res.shape
    k = a.shape[1]
    hid = w_up.shape[2]
    row = lambda i: (i, 0)
    return pl.pallas_call(
        functools.partial(_mlp_kernel, h_chunk=D_MODEL, final_norm=final_norm),
        grid=(t // ROW_TILE,),
        in_specs=[
            pl.BlockSpec((ROW_TILE, k), row),
            _resident((None, k, d), lambda i: (out_layer, 0, 0)),
            pl.BlockSpec((ROW_TILE, d), row),
            _resident((1, d), lambda i: (0, 0)),
            _resident((None, d, hid), lambda i: (layer, 0, 0)),
            _resident((None, hid, d), lambda i: (layer, 0, 0)),
            _resident((1, d), lambda i: (0, 0)),
        ],
        out_specs=pl.BlockSpec((ROW_TILE, d), row),
        out_shape=jax.ShapeDtypeStruct((t, d), F32),
        compiler_params=_params("parallel"),
        name="out_proj_mlp",
    )(a, w_out, res, g.reshape(1, d), w_up, w_down, g_out.reshape(1, d))


SB_DONE_SUM = 88.0
SB_HEADS_PER_LANE_BLOCK = LANES // SB_HEAD_DIM


SB_ROWS = SB_HEADS_PER_LANE_BLOCK * SB_TQ


def _sb_scores(q, keys):
    return lax.dot_general(q, keys, (((1,), (1,)), ((), ())),
                           preferred_element_type=F32)


def _sb_key_block(zs, vblks, tri, rs, mask):
    sps, log_sigs = [], []
    for z in zs:
        log1pe = jnp.log(1.0 + jnp.exp2(jnp.abs(z) * (-LOG2_E)))
        sp = jnp.maximum(z, 0.0) + log1pe
        log_sigs.append(z - sp)
        sps.append(sp if mask is None else jnp.where(mask, sp, 0.0))
    c_all = jnp.dot(jnp.concatenate([sp.astype(BF16) for sp in sps], axis=0), tri,
                    preferred_element_type=F32)
    pvs, rs_new = [], []
    for lb, (sp, log_sig, vblk) in enumerate(zip(sps, log_sigs, vblks)):
        c = c_all[lb * SB_ROWS:(lb + 1) * SB_ROWS]
        if rs is not None:
            c = c + jnp.concatenate([rs[lb]] * (SB_KB // LANES), axis=1)
        a = jnp.exp(log_sig - c)
        if mask is not None:
            a = jnp.where(mask, a, 0.0)
        pvs.append(jnp.dot(a.astype(BF16), vblk, preferred_element_type=F32))
        rs_new.append(jnp.broadcast_to(c[:, 0:1] + sp[:, 0:1], (SB_ROWS, LANES)))
    return pvs, rs_new


def _sb_attn_kernel(q_ref, k_ref, v_ref, tri_ref, o_ref, acc_ref, r_ref):
    tq, kb = SB_TQ, SB_KB
    i = pl.program_id(2)
    lane = lax.broadcasted_iota(jnp.int32, (tq, LANES), 1)
    tri = tri_ref[...]
    lane_blocks = range(SB_LANE_BLOCKS)

    qs = []
    for lb in lane_blocks:
        qf = q_ref[:, lb * LANES:(lb + 1) * LANES].astype(F32) * (SB_HEAD_DIM ** -0.5)
        per_head = []
        for h2 in range(SB_HEADS_PER_LANE_BLOCK):
            in_head = (lane >= h2 * SB_HEAD_DIM) & (lane < (h2 + 1) * SB_HEAD_DIM)
            per_head.append(jnp.where(in_head, qf, 0.0).astype(BF16))
        qs.append(jnp.concatenate(per_head, axis=0))

    def keys(ref, kstart, n_keys):
        return [ref[pl.ds(kstart, n_keys), lb * LANES:(lb + 1) * LANES] for lb in lane_blocks]

    def diagonal_blocks(with_previous):
        row = lax.broadcasted_iota(jnp.int32, (SB_ROWS, kb), 0) & (tq - 1)
        col = lax.broadcasted_iota(jnp.int32, (SB_ROWS, kb), 1)
        diag = pl.multiple_of(i * kb, kb)
        if with_previous:
            prev = pl.multiple_of((i - 1) * kb, kb)
            zs = [_sb_scores(q, k2) for q, k2 in zip(qs, keys(k_ref, prev, 2 * kb))]
            zs_prev = [z[:, :kb] for z in zs]
            zs = [z[:, kb:] for z in zs]
        else:
            zs = [_sb_scores(q, k) for q, k in zip(qs, keys(k_ref, diag, kb))]
        pvs, rs = _sb_key_block(zs, keys(v_ref, diag, kb), tri, None, col < row)
        if with_previous:
            pvs2, rs = _sb_key_block(zs_prev, keys(v_ref, prev, kb), tri, rs, None)
            pvs = [pv + pv2 for pv, pv2 in zip(pvs, pvs2)]
        for lb in lane_blocks:
            acc_ref[lb] = pvs[lb]
            r_ref[lb] = rs[lb]

    @pl.when(i == 0)
    def _():
        diagonal_blocks(False)

    @pl.when(i > 0)
    def _():
        diagonal_blocks(True)

    def min_sum():
        return jnp.min(r_ref[...])

    def cond(carry):
        j, m = carry
        return jnp.logical_and(j >= 0, m < SB_DONE_SUM)

    def body(carry):
        j, _ = carry
        kstart = pl.multiple_of(j * kb, kb)
        zs = [_sb_scores(q, k) for q, k in zip(qs, keys(k_ref, kstart, kb))]
        pvs, rs = _sb_key_block(zs, keys(v_ref, kstart, kb), tri,
                                [r_ref[lb] for lb in lane_blocks], None)
        for lb in lane_blocks:
            acc_ref[lb] += pvs[lb]
            r_ref[lb] = rs[lb]
        return j - 1, min_sum()

    lax.while_loop(cond, body, (i - 2, min_sum()))

    for lb in lane_blocks:
        acc = acc_ref[lb]
        o_ref[:, lb * LANES:(lb + 1) * LANES] = jnp.where(
            lane < SB_HEAD_DIM, acc[:tq], acc[tq:]).astype(o_ref.dtype)


def _sb_attention(qkv, batch, seq):
    t = qkv.shape[0]
    nq = seq // SB_TQ
    width = SB_LANE_BLOCKS * LANES
    groups = D_MODEL // width
    j = jnp.arange(SB_KB)
    tri = (j[:, None] > j[None, :]).astype(BF16)
    return pl.pallas_call(
        _sb_attn_kernel,
        grid=(batch, groups, nq),
        in_specs=[
            pl.BlockSpec((SB_TQ, width), lambda b, p, i: (b * nq + i, p)),
            pl.BlockSpec((seq, width), lambda b, p, i: (b, groups + p)),
            pl.BlockSpec((seq, width), lambda b, p, i: (b, 2 * groups + p)),
            _resident((SB_KB, SB_KB), lambda b, p, i: (0, 0)),
        ],
        out_specs=pl.BlockSpec((SB_TQ, width), lambda b, p, i: (b * nq + i, p)),
        out_shape=jax.ShapeDtypeStruct((t, D_MODEL), BF16),
        scratch_shapes=[
            pltpu.VMEM((SB_LANE_BLOCKS, SB_ROWS, LANES), F32),
            pltpu.VMEM((SB_LANE_BLOCKS, SB_ROWS, LANES), F32),
        ],
        compiler_params=_params("parallel", "parallel", "arbitrary"),
        name="sb_attention",
    )(qkv, qkv, qkv, tri)


def _retention_kernel(q_ref, k_ref, kd_ref, v_ref, sg_ref, din_ref, qd_ref, cd_ref,
                      o_ref, state_ref):
    n = pl.program_id(2)

    @pl.when(n == 0)
    def _():
        state_ref[...] = jnp.zeros_like(state_ref)

    for hh in range(RET_HEADS_PER_STEP):
        qk_cols = slice(hh * RET_QK_DIM, (hh + 1) * RET_QK_DIM)
        v_cols = slice(hh * RET_V_DIM, (hh + 1) * RET_V_DIM)
        qb = q_ref[:, qk_cols]
        vb = v_ref[:, v_cols]

        scores = lax.dot_general(qb, k_ref[:, qk_cols], (((1,), (1,)), ((), ())),
                                 preferred_element_type=F32) * din_ref[hh]
        inner = jnp.dot(scores.astype(BF16), vb, preferred_element_type=F32)
        state = state_ref[hh]
        cross = jnp.dot(qb, state.astype(BF16), preferred_element_type=F32) * qd_ref[hh]
        state_ref[hh] = state * cd_ref[hh] + lax.dot_general(
            kd_ref[:, qk_cols], vb, (((0,), (0,)), ((), ())), preferred_element_type=F32)

        o = inner + cross
        mu = jnp.mean(o, axis=-1, keepdims=True)
        xc = o - mu
        var = jnp.mean(xc * xc, axis=-1, keepdims=True)
        o = xc * lax.rsqrt(var + GN_EPS)
        o_ref[:, v_cols] = (sg_ref[:, v_cols].astype(F32) * o).astype(o_ref.dtype)


def _retention(qk, v, sg, decay_in, qd, cd, batch, seq):
    t = qk.shape[0]
    c = RET_BLOCK
    nc = seq // c
    hs = RET_HEADS_PER_STEP
    groups = RET_HEADS // hs
    qk_w, v_w = hs * RET_QK_DIM, hs * RET_V_DIM
    return pl.pallas_call(
        _retention_kernel,
        grid=(batch, groups, nc),
        in_specs=[
            pl.BlockSpec((c, qk_w), lambda b, h, n: (b * nc + n, h)),
            pl.BlockSpec((c, qk_w), lambda b, h, n: (b * nc + n, groups + h)),
            pl.BlockSpec((c, qk_w), lambda b, h, n: (b * nc + n, 2 * groups + h)),
            pl.BlockSpec((c, v_w), lambda b, h, n: (b * nc + n, h)),
            pl.BlockSpec((c, v_w), lambda b, h, n: (b * nc + n, h)),
            pl.BlockSpec((hs, c, c), lambda b, h, n: (h, 0, 0)),
            pl.BlockSpec((hs, c, RET_V_DIM), lambda b, h, n: (h, 0, 0)),
            pl.BlockSpec((hs, 1, RET_V_DIM), lambda b, h, n: (h, 0, 0)),
        ],
        out_specs=pl.BlockSpec((c, v_w), lambda b, h, n: (b * nc + n, h)),
        out_shape=jax.ShapeDtypeStruct((t, RET_HEADS * RET_V_DIM), BF16),
        scratch_shapes=[pltpu.VMEM((hs, RET_QK_DIM, RET_V_DIM), F32)],
        compiler_params=_params("parallel", "parallel", "arbitrary"),
        name="retention",
    )(qk, qk, qk, v, sg, decay_in, qd, cd)


def kernel(x, w_sb_in, w_sb_out, w_ret_in, w_ret_out, g_mix, g_mlp, w_mlp_up,
           w_mlp_down, g_final):
    batch, seq, d = x.shape
    depth = g_mix.shape[0]
    w_sb_in, w_sb_out, w_ret_in, w_ret_out, w_mlp_up, w_mlp_down = (
        _cast_bf16(w) for w in (w_sb_in, w_sb_out, w_ret_in, w_ret_out,
                                w_mlp_up, w_mlp_down))
    cos, sin, decay_in, qd, kd, cd = _retention_constants(seq)
    h = x.reshape(batch * seq, d)
    for layer in range(depth):
        i = layer // N_MIXERS
        if layer % N_MIXERS == 0:
            qkv = _sb_proj(h, g_mix[layer], w_sb_in, i)
            mixed = _sb_attention(qkv, batch, seq)
            w_out = w_sb_out
        else:
            qk, v, sg = _ret_proj(h, g_mix[layer], w_ret_in, i, cos, sin, kd, seq)
            mixed = _retention(qk, v, sg, decay_in, qd, cd, batch, seq)
            w_out = w_ret_out
        h = _out_proj_mlp(mixed, w_out, i, h, g_mlp[layer], w_mlp_up, w_mlp_down,
                          layer, g_final, layer == depth - 1)
    return h.reshape(batch, seq, d)
```

```python
import functools

import jax
import jax.numpy as jnp
from jax import lax
from jax.experimental import pallas as pl
from jax.experimental.pallas import tpu as pltpu

F32 = jnp.float32
BF16 = jnp.bfloat16

D_MODEL = 1024
N_MIXERS = 2
SB_HEADS = 16
SB_HEAD_DIM = D_MODEL // SB_HEADS
RET_HEADS = 4
RET_QK_DIM = D_MODEL // RET_HEADS
RET_V_DIM = 2 * D_MODEL // RET_HEADS
ROPE_BASE = 10000.0
MLP_HIDDEN = 4 * D_MODEL
RMS_EPS = 1e-6
GN_EPS = 1e-6
LOG2_E = 1.4426950408889634

LANES = 128
VMEM_LIMIT = 56 * 1024 * 1024

ROW_TILE = 512
CAST_BLOCK_ELEMS = 1 << 20
SB_TQ = 256
SB_KB = 256
SB_LANE_BLOCKS = 8
RET_BLOCK = 256
RET_HEADS_PER_STEP = 4


def _rms_norm_f32(x, g):
    ms = jnp.mean(x * x, axis=-1, keepdims=True)
    return x * lax.rsqrt(ms + RMS_EPS) * g


def _resident(block_shape, index_map):
    return pl.BlockSpec(block_shape, index_map, pipeline_mode=pl.Buffered(1))


def _params(*semantics):
    return pltpu.CompilerParams(dimension_semantics=semantics,
                                vmem_limit_bytes=VMEM_LIMIT)


def _cast_kernel(x_ref, o_ref):
    o_ref[...] = x_ref[...].astype(o_ref.dtype)


def _cast_bf16(w):
    layers, k, n = w.shape
    rows = min(k, 1 << ((CAST_BLOCK_ELEMS // n).bit_length() - 1))
    spec = pl.BlockSpec((None, rows, n), lambda a, r: (a, r, 0))
    return pl.pallas_call(
        _cast_kernel,
        grid=(layers, k // rows),
        in_specs=[spec],
        out_specs=spec,
        out_shape=jax.ShapeDtypeStruct(w.shape, BF16),
        compiler_params=_params("parallel", "parallel"),
        name="cast_bf16",
    )(w)


def _sb_proj_kernel(x_ref, g_ref, w_ref, o_ref):
    xn = _rms_norm_f32(x_ref[...], g_ref[...]).astype(BF16)
    for j in range(w_ref.shape[1] // D_MODEL):
        cols = slice(j * D_MODEL, (j + 1) * D_MODEL)
        o_ref[:, cols] = jnp.dot(
            xn, w_ref[:, cols], preferred_element_type=F32).astype(o_ref.dtype)


def _sb_proj(x, g, w, layer):
    t, d = x.shape
    n = w.shape[2]
    return pl.pallas_call(
        _sb_proj_kernel,
        grid=(t // ROW_TILE,),
        in_specs=[
            pl.BlockSpec((ROW_TILE, d), lambda i: (i, 0)),
            _resident((1, d), lambda i: (0, 0)),
            _resident((None, d, n), lambda i: (layer, 0, 0)),
        ],
        out_specs=pl.BlockSpec((ROW_TILE, n), lambda i: (i, 0)),
        out_shape=jax.ShapeDtypeStruct((t, n), BF16),
        compiler_params=_params("parallel"),
        name="sb_proj",
    )(x, g.reshape(1, d), w)


def _rotate(x, cos, sin):
    half = x.shape[-1] // 2
    x1, x2 = x[:, :half], x[:, half:]
    return jnp.concatenate([x1 * cos - x2 * sin, x1 * sin + x2 * cos], axis=1)


def _ret_proj_kernel(x_ref, g_ref, w_ref, cos_ref, sin_ref, kd_ref,
                     qk_ref, v_ref, sg_ref):
    xn = _rms_norm_f32(x_ref[...], g_ref[...]).astype(BF16)
    cos, sin = cos_ref[...], sin_ref[...]

    def proj(j):
        return jnp.dot(xn, w_ref[:, j * D_MODEL:(j + 1) * D_MODEL],
                       preferred_element_type=F32)

    q = proj(0)
    for hh in range(RET_HEADS):
        cols = slice(hh * RET_QK_DIM, (hh + 1) * RET_QK_DIM)
        qk_ref[:, cols] = _rotate(q[:, cols], cos, sin).astype(BF16)
    k = proj(1)
    for hh in range(RET_HEADS):
        cols = slice(hh * RET_QK_DIM, (hh + 1) * RET_QK_DIM)
        kr = _rotate(k[:, cols], cos, sin) * (RET_QK_DIM ** -0.5)
        qk_ref[:, D_MODEL + hh * RET_QK_DIM:D_MODEL + (hh + 1) * RET_QK_DIM] = kr.astype(BF16)
        qk_ref[:, 2 * D_MODEL + hh * RET_QK_DIM:2 * D_MODEL + (hh + 1) * RET_QK_DIM] = (
            kr * kd_ref[hh]).astype(BF16)
    for j in range(2):
        cols = slice(j * D_MODEL, (j + 1) * D_MODEL)
        v_ref[:, cols] = proj(2 + j).astype(BF16)
        gate = proj(4 + j)
        sg_ref[:, cols] = (gate * (1.0 / (1.0 + jnp.exp(-gate)))).astype(BF16)


def _retention_constants(seq):
    c = RET_BLOCK
    half = RET_QK_DIM // 2
    inv_freq = 1.0 / (ROPE_BASE ** jnp.linspace(0.0, 1.0, half, dtype=F32))
    ang = jnp.arange(seq).astype(F32)[:, None] * inv_freq[None, :]
    cos, sin = jnp.cos(ang), jnp.sin(ang)

    log_gamma = jnp.log1p(-jnp.exp2(-5.0 - jnp.arange(RET_HEADS, dtype=F32)))
    idx = jnp.arange(c, dtype=F32)
    rel = idx[:, None] - idx[None, :]
    decay_in = jnp.where(
        rel >= 0, jnp.exp(log_gamma[:, None, None] * jnp.maximum(rel, 0.0)), 0.0)
    q_decay = jnp.exp(log_gamma[:, None] * (idx + 1.0))
    k_decay = jnp.exp(log_gamma[:, None] * (c - 1.0 - idx))
    chunk_decay = jnp.exp(log_gamma * c)
    qd = jnp.broadcast_to(q_decay[:, :, None], (RET_HEADS, c, RET_V_DIM))
    k_decay_tile = jnp.tile(k_decay, (1, ROW_TILE // c))
    kd = jnp.broadcast_to(k_decay_tile[:, :, None], (RET_HEADS, ROW_TILE, RET_QK_DIM))
    cd = jnp.broadcast_to(chunk_decay[:, None, None], (RET_HEADS, 1, RET_V_DIM))
    return cos, sin, decay_in, qd, kd, cd


def _ret_proj(x, g, w, layer, cos, sin, kd, seq):
    t, d = x.shape
    n = w.shape[2]
    half = RET_QK_DIM // 2
    pos_blocks = seq // ROW_TILE
    row = lambda i: (i, 0)
    return pl.pallas_call(
        _ret_proj_kernel,
        grid=(t // ROW_TILE,),
        in_specs=[
            pl.BlockSpec((ROW_TILE, d), row),
            _resident((1, d), lambda i: (0, 0)),
            _resident((None, d, n), lambda i: (layer, 0, 0)),
            pl.BlockSpec((ROW_TILE, half), lambda i: (i % pos_blocks, 0)),
            pl.BlockSpec((ROW_TILE, half), lambda i: (i % pos_blocks, 0)),
            _resident((RET_HEADS, ROW_TILE, RET_QK_DIM), lambda i: (0, 0, 0)),
        ],
        out_specs=[
            pl.BlockSpec((ROW_TILE, 3 * d), row),
            pl.BlockSpec((ROW_TILE, 2 * d), row),
            pl.BlockSpec((ROW_TILE, 2 * d), row),
        ],
        out_shape=[
            jax.ShapeDtypeStruct((t, 3 * d), BF16),
            jax.ShapeDtypeStruct((t, 2 * d), BF16),
            jax.ShapeDtypeStruct((t, 2 * d), BF16),
        ],
        compiler_params=_params("parallel"),
        name="ret_proj",
    )(x, g.reshape(1, d), w, cos, sin, kd)


def _mlp_kernel(a_ref, wo_ref, r_ref, g_ref, wu_ref, wd_ref, go_ref, o_ref, *,
                h_chunk, final_norm):
    x = r_ref[...] + jnp.dot(a_ref[...], wo_ref[...], preferred_element_type=F32)
    xn = _rms_norm_f32(x, g_ref[...]).astype(BF16)
    y = x
    for j in range(wu_ref.shape[1] // h_chunk):
        cols = slice(j * h_chunk, (j + 1) * h_chunk)
        u = jnp.maximum(jnp.dot(xn, wu_ref[:, cols], preferred_element_type=F32), 0.0)
        y = y + jnp.dot((u * u).astype(BF16), wd_ref[cols, :],
                        preferred_element_type=F32)
    if final_norm:
        y = _rms_norm_f32(y, go_ref[...])
    o_ref[...] = y


def _out_proj_mlp(a, w_out, out_layer, res, g, w_up, w_down, layer, g_out, final_norm):
    t, d = res.shape
    k = a.shape[1]
    hid = w_up.shape[2]
    row = lambda i: (i, 0)
    return pl.pallas_call(
        functools.partial(_mlp_kernel, h_chunk=D_MODEL, final_norm=final_norm),
        grid=(t // ROW_TILE,),
        in_specs=[
            pl.BlockSpec((ROW_TILE, k), row),
            _resident((None, k, d), lambda i: (out_layer, 0, 0)),
            pl.BlockSpec((ROW_TILE, d), row),
            _resident((1, d), lambda i: (0, 0)),
            _resident((None, d, hid), lambda i: (layer, 0, 0)),
            _resident((None, hid, d), lambda i: (layer, 0, 0)),
            _resident((1, d), lambda i: (0, 0)),
        ],
        out_specs=pl.BlockSpec((ROW_TILE, d), row),
        out_shape=jax.ShapeDtypeStruct((t, d), F32),
        compiler_params=_params("parallel"),
        name="out_proj_mlp",
    )(a, w_out, res, g.reshape(1, d), w_up, w_down, g_out.reshape(1, d))


SB_DONE_SUM = 88.0
SB_HEADS_PER_LANE_BLOCK = LANES // SB_HEAD_DIM


SB_ROWS = SB_HEADS_PER_LANE_BLOCK * SB_TQ


def _sb_scores(q, keys):
    return lax.dot_general(q, keys, (((1,), (1,)), ((), ())),
                           preferred_element_type=F32)


def _sb_key_block(zs, vblks, tri, rs, mask):
    sps, log_sigs = [], []
    for z in zs:
        log1pe = jnp.log(1.0 + jnp.exp2(jnp.abs(z) * (-LOG2_E)))
        sp = jnp.maximum(z, 0.0) + log1pe
        log_sigs.append(z - sp)
        sps.append(sp if mask is None else jnp.where(mask, sp, 0.0))
    c_all = jnp.dot(jnp.concatenate([sp.astype(BF16) for sp in sps], axis=0), tri,
                    preferred_element_type=F32)
    pvs, rs_new = [], []
    for lb, (sp, log_sig, vblk) in enumerate(zip(sps, log_sigs, vblks)):
        c = c_all[lb * SB_ROWS:(lb + 1) * SB_ROWS]
        if rs is not None:
            c = c + jnp.concatenate([rs[lb]] * (SB_KB // LANES), axis=1)
        a = jnp.exp(log_sig - c)
        if mask is not None:
            a = jnp.where(mask, a, 0.0)
        pvs.append(jnp.dot(a.astype(BF16), vblk, preferred_element_type=F32))
        rs_new.append(jnp.broadcast_to(c[:, 0:1] + sp[:, 0:1], (SB_ROWS, LANES)))
    return pvs, rs_new


def _sb_attn_kernel(q_ref, k_ref, v_ref, tri_ref, o_ref, acc_ref, r_ref):
    tq, kb = SB_TQ, SB_KB
    i = pl.program_id(2)
    lane = lax.broadcasted_iota(jnp.int32, (tq, LANES), 1)
    tri = tri_ref[...]
    lane_blocks = range(SB_LANE_BLOCKS)

    qs = []
    for lb in lane_blocks:
        qf = q_ref[:, lb * LANES:(lb + 1) * LANES].astype(F32) * (SB_HEAD_DIM ** -0.5)
        per_head = []
        for h2 in range(SB_HEADS_PER_LANE_BLOCK):
            in_head = (lane >= h2 * SB_HEAD_DIM) & (lane < (h2 + 1) * SB_HEAD_DIM)
            per_head.append(jnp.where(in_head, qf, 0.0).astype(BF16))
        qs.append(jnp.concatenate(per_head, axis=0))

    def keys(ref, kstart, n_keys):
        return [ref[pl.ds(kstart, n_keys), lb * LANES:(lb + 1) * LANES] for lb in lane_blocks]

    def diagonal_blocks(with_previous):
        row = lax.broadcasted_iota(jnp.int32, (SB_ROWS, kb), 0) & (tq - 1)
        col = lax.broadcasted_iota(jnp.int32, (SB_ROWS, kb), 1)
        diag = pl.multiple_of(i * kb, kb)
        if with_previous:
            prev = pl.multiple_of((i - 1) * kb, kb)
            zs = [_sb_scores(q, k2) for q, k2 in zip(qs, keys(k_ref, prev, 2 * kb))]
            zs_prev = [z[:, :kb] for z in zs]
            zs = [z[:, kb:] for z in zs]
        else:
            zs = [_sb_scores(q, k) for q, k in zip(qs, keys(k_ref, diag, kb))]
        pvs, rs = _sb_key_block(zs, keys(v_ref, diag, kb), tri, None, col < row)
        if with_previous:
            pvs2, rs = _sb_key_block(zs_prev, keys(v_ref, prev, kb), tri, rs, None)
            pvs = [pv + pv2 for pv, pv2 in zip(pvs, pvs2)]
        for lb in lane_blocks:
            acc_ref[lb] = pvs[lb]
            r_ref[lb] = rs[lb]

    @pl.when(i == 0)
    def _():
        diagonal_blocks(False)

    @pl.when(i > 0)
    def _():
        diagonal_blocks(True)

    def min_sum():
        return jnp.min(r_ref[...])

    def cond(carry):
        j, m = carry
        return jnp.logical_and(j >= 0, m < SB_DONE_SUM)

    def body(carry):
        j, _ = carry
        kstart = pl.multiple_of(j * kb, kb)
        zs = [_sb_scores(q, k) for q, k in zip(qs, keys(k_ref, kstart, kb))]
        pvs, rs = _sb_key_block(zs, keys(v_ref, kstart, kb), tri,
                                [r_ref[lb] for lb in lane_blocks], None)
        for lb in lane_blocks:
            acc_ref[lb] += pvs[lb]
            r_ref[lb] = rs[lb]
        return j - 1, min_sum()

    lax.while_loop(cond, body, (i - 2, min_sum()))

    for lb in lane_blocks:
        acc = acc_ref[lb]
        o_ref[:, lb * LANES:(lb + 1) * LANES] = jnp.where(
            lane < SB_HEAD_DIM, acc[:tq], acc[tq:]).astype(o_ref.dtype)


def _sb_attention(qkv, batch, seq):
    t = qkv.shape[0]
    nq = seq // SB_TQ
    width = SB_LANE_BLOCKS * LANES
    groups = D_MODEL // width
    j = jnp.arange(SB_KB)
    tri = (j[:, None] > j[None, :]).astype(BF16)
    return pl.pallas_call(
        _sb_attn_kernel,
        grid=(batch, groups, nq),
        in_specs=[
            pl.BlockSpec((SB_TQ, width), lambda b, p, i: (b * nq + i, p)),
            pl.BlockSpec((seq, width), lambda b, p, i: (b, groups + p)),
            pl.BlockSpec((seq, width), lambda b, p, i: (b, 2 * groups + p)),
            _resident((SB_KB, SB_KB), lambda b, p, i: (0, 0)),
        ],
        out_specs=pl.BlockSpec((SB_TQ, width), lambda b, p, i: (b * nq + i, p)),
        out_shape=jax.ShapeDtypeStruct((t, D_MODEL), BF16),
        scratch_shapes=[
            pltpu.VMEM((SB_LANE_BLOCKS, SB_ROWS, LANES), F32),
            pltpu.VMEM((SB_LANE_BLOCKS, SB_ROWS, LANES), F32),
        ],
        compiler_params=_params("parallel", "parallel", "arbitrary"),
        name="sb_attention",
    )(qkv, qkv, qkv, tri)


def _retention_kernel(q_ref, k_ref, kd_ref, v_ref, sg_ref, din_ref, qd_ref, cd_ref,
                      o_ref, state_ref):
    n = pl.program_id(2)

    @pl.when(n == 0)
    def _():
        state_ref[...] = jnp.zeros_like(state_ref)

    for hh in range(RET_HEADS_PER_STEP):
        qk_cols = slice(hh * RET_QK_DIM, (hh + 1) * RET_QK_DIM)
        v_cols = slice(hh * RET_V_DIM, (hh + 1) * RET_V_DIM)
        qb = q_ref[:, qk_cols]
        vb = v_ref[:, v_cols]

        scores = lax.dot_general(qb, k_ref[:, qk_cols], (((1,), (1,)), ((), ())),
                                 preferred_element_type=F32) * din_ref[hh]
        inner = jnp.dot(scores.astype(BF16), vb, preferred_element_type=F32)
        state = state_ref[hh]
        cross = jnp.dot(qb, state.astype(BF16), preferred_element_type=F32) * qd_ref[hh]
        state_ref[hh] = state * cd_ref[hh] + lax.dot_general(
            kd_ref[:, qk_cols], vb, (((0,), (0,)), ((), ())), preferred_element_type=F32)

        o = inner + cross
        mu = jnp.mean(o, axis=-1, keepdims=True)
        xc = o - mu
        var = jnp.mean(xc * xc, axis=-1, keepdims=True)
        o = xc * lax.rsqrt(var + GN_EPS)
        o_ref[:, v_cols] = (sg_ref[:, v_cols].astype(F32) * o).astype(o_ref.dtype)


def _retention(qk, v, sg, decay_in, qd, cd, batch, seq):
    t = qk.shape[0]
    c = RET_BLOCK
    nc = seq // c
    hs = RET_HEADS_PER_STEP
    groups = RET_HEADS // hs
    qk_w, v_w = hs * RET_QK_DIM, hs * RET_V_DIM
    return pl.pallas_call(
        _retention_kernel,
        grid=(batch, groups, nc),
        in_specs=[
            pl.BlockSpec((c, qk_w), lambda b, h, n: (b * nc + n, h)),
            pl.BlockSpec((c, qk_w), lambda b, h, n: (b * nc + n, groups + h)),
            pl.BlockSpec((c, qk_w), lambda b, h, n: (b * nc + n, 2 * groups + h)),
            pl.BlockSpec((c, v_w), lambda b, h, n: (b * nc + n, h)),
            pl.BlockSpec((c, v_w), lambda b, h, n: (b * nc + n, h)),
            pl.BlockSpec((hs, c, c), lambda b, h, n: (h, 0, 0)),
            pl.BlockSpec((hs, c, RET_V_DIM), lambda b, h, n: (h, 0, 0)),
            pl.BlockSpec((hs, 1, RET_V_DIM), lambda b, h, n: (h, 0, 0)),
        ],
        out_specs=pl.BlockSpec((c, v_w), lambda b, h, n: (b * nc + n, h)),
        out_shape=jax.ShapeDtypeStruct((t, RET_HEADS * RET_V_DIM), BF16),
        scratch_shapes=[pltpu.VMEM((hs, RET_QK_DIM, RET_V_DIM), F32)],
        compiler_params=_params("parallel", "parallel", "arbitrary"),
        name="retention",
    )(qk, qk, qk, v, sg, decay_in, qd, cd)


def kernel(x, w_sb_in, w_sb_out, w_ret_in, w_ret_out, g_mix, g_mlp, w_mlp_up,
           w_mlp_down, g_final):
    batch, seq, d = x.shape
    depth = g_mix.shape[0]
    w_sb_in, w_sb_out, w_ret_in, w_ret_out, w_mlp_up, w_mlp_down = (
        _cast_bf16(w) for w in (w_sb_in, w_sb_out, w_ret_in, w_ret_out,
                                w_mlp_up, w_mlp_down))
    cos, sin, decay_in, qd, kd, cd = _retention_constants(seq)
    h = x.reshape(batch * seq, d)
    for layer in range(depth):
        i = layer // N_MIXERS
        if layer % N_MIXERS == 0:
            qkv = _sb_proj(h, g_mix[layer], w_sb_in, i)
            mixed = _sb_attention(qkv, batch, seq)
            w_out = w_sb_out
        else:
            qk, v, sg = _ret_proj(h, g_mix[layer], w_ret_in, i, cos, sin, kd, seq)
            mixed = _retention(qk, v, sg, decay_in, qd, cd, batch, seq)
            w_out = w_ret_out
        h = _out_proj_mlp(mixed, w_out, i, h, g_mlp[layer], w_mlp_up, w_mlp_down,
                          layer, g_final, layer == depth - 1)
    return h.reshape(batch, seq, d)
```
